```python
import jax, jax.numpy as jnp
from jax import lax
import numpy as np

D_MODEL = 2048
BATCH = 4
SEQ = 4096
DEPTH = 1

N_ATTN_HEADS = 8
HEAD_DIM = 128
ATTN_WIDTH = N_ATTN_HEADS * HEAD_DIM
POOL_WINDOWS = (2, 4, 8, 16)
N_POOL_GROUPS = len(POOL_WINDOWS)
POOL_GROUP_WIDTH = 256
POOL_WIDTH = N_POOL_GROUPS * POOL_GROUP_WIDTH
N_BRANCHES = 2
IN_WIDTH = 3 * ATTN_WIDTH + POOL_WIDTH + N_BRANCHES * D_MODEL
D_FF = 5632
CONV_WIDTH = 3
PLE_DIM = 256
Q_BLOCK = 128
EPS = 1e-6

kernel_name = "hybrid_stickbreak_pool_convffn_layer"


def rmsnorm(x, gain):
    xf = x.astype(jnp.float32)
    y = xf * lax.rsqrt(jnp.mean(xf * xf, axis=-1, keepdims=True) + EPS)
    return (y * gain.astype(jnp.float32)).astype(x.dtype)


def stick_breaking_attention(q, k, v):
    B, S, H, Dh = q.shape
    nb = S // Q_BLOCK
    scale = Dh ** -0.5
    qb = q.reshape(B, nb, Q_BLOCK, H, Dh).transpose(1, 0, 3, 2, 4)
    starts = jnp.arange(nb, dtype=jnp.int32) * Q_BLOCK
    key_pos = jnp.arange(S, dtype=jnp.int32)

    def block(args):
        q_i, start = args
        z = jnp.einsum('bhqd,bkhd->bhqk', q_i, k).astype(jnp.float32) * scale
        q_pos = start + jnp.arange(Q_BLOCK, dtype=jnp.int32)
        causal = key_pos[None, :] < q_pos[:, None]
        log_1m_beta = jnp.where(causal, jax.nn.log_sigmoid(-z), 0.0)
        suffix = lax.cumsum(log_1m_beta, axis=3, reverse=True) - log_1m_beta
        log_a = jax.nn.log_sigmoid(z) + suffix
        a = jnp.where(causal, jnp.exp(log_a), 0.0)
        return jnp.einsum('bhqk,bkhd->bqhd', a.astype(v.dtype), v)

    out = lax.map(block, (qb, starts))
    return out.transpose(1, 0, 2, 3, 4).reshape(B, S, H * Dh)


def multiscale_causal_pool(u):
    B, S, _ = u.shape
    groups = u.astype(jnp.float32).reshape(B, S, N_POOL_GROUPS, POOL_GROUP_WIDTH)
    csum = jnp.pad(jnp.cumsum(groups, axis=1), ((0, 0), (1, 0), (0, 0), (0, 0)))
    pos = jnp.arange(S, dtype=jnp.int32)
    means = []
    for g, w in enumerate(POOL_WINDOWS):
        upper = csum[:, 1:, g]
        lower = jnp.pad(csum[:, :S + 1 - w, g], ((0, 0), (w - 1, 0), (0, 0)))
        count = jnp.minimum(pos + 1, w).astype(jnp.float32)[None, :, None]
        means.append((upper - lower) / count)
    return jnp.stack(means, axis=2) - groups


def causal_depthwise_conv(h, w, b):
    S = h.shape[1]
    hp = jnp.pad(h, ((0, 0), (CONV_WIDTH - 1, 0), (0, 0)))
    out = b
    for j in range(CONV_WIDTH):
        out = out + hp[:, j:j + S] * w[j]
    return out


def setup_inputs(seed: int = 0) -> dict:
    key = jax.random.key(seed)
    ks = jax.random.split(key, 20)
    f32 = jnp.float32

    def w(k, shape, fan_in):
        return jax.random.normal(k, shape, f32) * (fan_in ** -0.5)

    def gain(k, shape):
        return 1.0 + 0.02 * jax.random.normal(k, shape, f32)

    return {
        "x": jax.random.normal(ks[0], (BATCH, SEQ, D_MODEL), f32),
        "p": jax.random.normal(ks[1], (DEPTH, BATCH, SEQ, PLE_DIM), f32),
        "norm_mix_pre": gain(ks[2], (DEPTH, D_MODEL)),
        "w_in": w(ks[3], (DEPTH, D_MODEL, IN_WIDTH), D_MODEL),
        "w_attn_branch": w(ks[4], (DEPTH, ATTN_WIDTH, D_MODEL), ATTN_WIDTH),
        "w_pool_group": w(ks[5], (DEPTH, N_POOL_GROUPS, POOL_GROUP_WIDTH, POOL_GROUP_WIDTH), POOL_GROUP_WIDTH),
        "pool_scale": gain(ks[6], (DEPTH, POOL_WIDTH)),
        "w_pool_branch": w(ks[7], (DEPTH, POOL_WIDTH, D_MODEL), POOL_WIDTH),
        "w_out": w(ks[8], (DEPTH, D_MODEL, D_MODEL), D_MODEL),
        "norm_mix_post": gain(ks[9], (DEPTH, D_MODEL)),
        "norm_ffn_pre": gain(ks[10], (DEPTH, D_MODEL)),
        "w_up": w(ks[11], (DEPTH, D_MODEL, 2 * D_FF), D_MODEL),
        "conv_w": w(ks[12], (DEPTH, CONV_WIDTH, 2 * D_FF), CONV_WIDTH),
        "conv_b": 0.01 * jax.random.normal(ks[13], (DEPTH, 2 * D_FF), f32),
        "w_down": w(ks[14], (DEPTH, D_FF, D_MODEL), D_FF),
        "norm_ffn_post": gain(ks[15], (DEPTH, D_MODEL)),
        "w_ple": w(ks[16], (DEPTH, PLE_DIM, D_MODEL), PLE_DIM),
        "w_ple_gate": w(ks[17], (DEPTH, D_MODEL, D_MODEL), D_MODEL),
        "norm_ple_post": gain(ks[18], (DEPTH, D_MODEL)),
    }


def reference(x, p, norm_mix_pre, w_in, w_attn_branch, w_pool_group, pool_scale, w_pool_branch, w_out,
              norm_mix_post, norm_ffn_pre, w_up, conv_w, conv_b, w_down, norm_ffn_post, w_ple, w_ple_gate,
              norm_ple_post):
    B, S, _ = x.shape
    splits = [ATTN_WIDTH, 2 * ATTN_WIDTH, 3 * ATTN_WIDTH, 3 * ATTN_WIDTH + POOL_WIDTH,
              3 * ATTN_WIDTH + POOL_WIDTH + D_MODEL]
    for i in range(DEPTH):
        h = rmsnorm(x, norm_mix_pre[i])
        proj = h @ w_in[i]
        q, k, v, u, g_attn, g_pool = jnp.split(proj, splits, axis=-1)
        q = q.reshape(B, S, N_ATTN_HEADS, HEAD_DIM)
        k = k.reshape(B, S, N_ATTN_HEADS, HEAD_DIM)
        v = v.reshape(B, S, N_ATTN_HEADS, HEAD_DIM)
        y_attn = stick_breaking_attention(q, k, v) @ w_attn_branch[i]

        pooled = multiscale_causal_pool(u).astype(u.dtype)
        pooled = jnp.einsum('bsgc,gcd->bsgd', pooled, w_pool_group[i]).reshape(B, S, POOL_WIDTH)
        y_pool = (pooled * pool_scale[i]) @ w_pool_branch[i]

        mixed = jax.nn.sigmoid(g_attn) * y_attn + jax.nn.sigmoid(g_pool) * y_pool
        x = x + rmsnorm(mixed @ w_out[i], norm_mix_post[i])

        h = rmsnorm(x, norm_ffn_pre[i])
        up = causal_depthwise_conv(h @ w_up[i], conv_w[i], conv_b[i])
        gate, val = jnp.split(up, 2, axis=-1)
        y_ffn = (jax.nn.gelu(gate, approximate=True) * val) @ w_down[i]
        x = x + rmsnorm(y_ffn, norm_ffn_post[i])

        e = p[i] @ w_ple[i]
        x = x + rmsnorm(jax.nn.sigmoid(x @ w_ple_gate[i]) * e, norm_ple_post[i])
    return x
```

```python
import functools

import jax
import jax.numpy as jnp
from jax import lax
from jax.experimental import pallas as pl
from jax.experimental.pallas import tpu as pltpu

N_ATTN_HEADS = 8
HEAD_DIM = 128
ATTN_WIDTH = N_ATTN_HEADS * HEAD_DIM
POOL_WINDOWS = (2, 4, 8, 16)
POOL_GROUP_WIDTH = 256
POOL_WIDTH = len(POOL_WINDOWS) * POOL_GROUP_WIDTH
CONV_WIDTH = 3
EPS = 1e-6

HALO = 16

VMEM_LIMIT_BYTES = 56 * 1024 * 1024

F32 = jnp.float32
BF16 = jnp.bfloat16


def _rmsnorm(x, gain):
    return x * lax.rsqrt(jnp.mean(x * x, axis=-1, keepdims=True) + EPS) * gain


def _params(*semantics):
    return pltpu.CompilerParams(dimension_semantics=semantics, vmem_limit_bytes=VMEM_LIMIT_BYTES)


def _in_proj_kernel(x_ref, g_ref, w_ref, o_ref, h_ref):
    @pl.when(pl.program_id(1) == 0)
    def _():
        h_ref[...] = _rmsnorm(x_ref[...], g_ref[...]).astype(BF16)

    o_ref[...] = jnp.dot(h_ref[...], w_ref[...], preferred_element_type=F32).astype(o_ref.dtype)


def _in_proj(x2d, gain, w_bf16, *, tm=1024, tn=1024):
    t, d = x2d.shape
    n = w_bf16.shape[1]
    return pl.pallas_call(
        _in_proj_kernel,
        out_shape=jax.ShapeDtypeStruct((t, n), BF16),
        grid=(t // tm, n // tn),
        in_specs=[
            pl.BlockSpec((tm, d), lambda i, j: (i, 0)),
            pl.BlockSpec((1, d), lambda i, j: (0, 0)),
            pl.BlockSpec((d, tn), lambda i, j: (0, j)),
        ],
        out_specs=pl.BlockSpec((tm, tn), lambda i, j: (i, j)),
        scratch_shapes=[pltpu.VMEM((tm, d), BF16)],
        compiler_params=_params("parallel", "arbitrary"),
        name="in_proj",
    )(x2d, gain, w_bf16)


def _attn_kernel(q_ref, k_ref, v_ref, o_ref, acc_ref, carry_ref, *, tq, tk, scale):
    qi = pl.program_id(2)
    q = q_ref[...]

    r = lax.broadcasted_iota(jnp.int32, (2 * tk, 2 * tk), 0)
    c = lax.broadcasted_iota(jnp.int32, (2 * tk, 2 * tk), 1)
    r = jnp.where(r >= tk, r - tk, r)
    sum_mat = jnp.where(c >= tk, 1.0, jnp.where(r > c, 1.0, 0.0)).astype(BF16)

    acc_ref[...] = jnp.zeros_like(acc_ref)
    carry_ref[...] = jnp.zeros_like(carry_ref)

    def block(kb, masked):
        ks = pl.multiple_of(kb * tk, tk)
        k = k_ref[pl.ds(ks, tk), :]
        v = v_ref[pl.ds(ks, tk), :]
        z = lax.dot_general(q, k, (((1,), (1,)), ((), ())), preferred_element_type=F32) * scale
        sp = jnp.maximum(z, 0.0) + jnp.log1p(jnp.exp(-jnp.abs(z)))
        if masked:
            t_pos = qi * tq + lax.broadcasted_iota(jnp.int32, (tq, tk), 0)
            s_pos = ks + lax.broadcasted_iota(jnp.int32, (tq, tk), 1)
            causal = s_pos < t_pos
            nl = jnp.where(causal, sp, 0.0)
        else:
            nl = sp
        hi = nl.astype(BF16)
        lo = (nl - hi.astype(F32)).astype(BF16)
        sums = jnp.dot(jnp.concatenate([hi, lo], axis=1), sum_mat, preferred_element_type=F32)
        log_a = z - sp - sums[:, :tk] - carry_ref[...]
        a = jnp.exp(log_a)
        if masked:
            a = jnp.where(causal, a, 0.0)
        acc_ref[...] += jnp.dot(a.astype(BF16), v, preferred_element_type=F32)
        carry_ref[...] += sums[:, tk:]

    n_diag = tq // tk
    first_kb = (qi + 1) * n_diag - 1
    for d in range(n_diag):
        block(first_kb - d, masked=True)

    def body(j, _):
        block(qi * n_diag - 1 - j, masked=False)
        return 0

    lax.fori_loop(0, qi * n_diag, body, 0)
    o_ref[...] = acc_ref[...].astype(o_ref.dtype)


def _attention(proj, batch, seq, *, tq=256, tk=128):
    t = proj.shape[0]
    nq = seq // tq
    kern = functools.partial(_attn_kernel, tq=tq, tk=tk, scale=HEAD_DIM ** -0.5)
    return pl.pallas_call(
        kern,
        out_shape=jax.ShapeDtypeStruct((t, ATTN_WIDTH), BF16),
        grid=(batch, N_ATTN_HEADS, nq),
        in_specs=[
            pl.BlockSpec((tq, HEAD_DIM), lambda b, h, i: (b * nq + i, h)),
            pl.BlockSpec((seq, HEAD_DIM), lambda b, h, i: (b, N_ATTN_HEADS + h)),
            pl.BlockSpec((seq, HEAD_DIM), lambda b, h, i: (b, 2 * N_ATTN_HEADS + h)),
        ],
        out_specs=pl.BlockSpec((tq, HEAD_DIM), lambda b, h, i: (b * nq + i, h)),
        scratch_shapes=[pltpu.VMEM((tq, HEAD_DIM), F32), pltpu.VMEM((tq, tk), F32)],
        compiler_params=_params("parallel", "parallel", "arbitrary"),
        name="attn",
    )(proj, proj, proj)


def _mix_kernel(attn_ref, u_ref, uh_ref, ga_ref, gp_ref, x_ref, wab_ref, wpg_ref, ps_ref, wpb_ref,
                wout_ref, gpost_ref, o_ref, ucat_ref, *, tm, seq):
    i = pl.program_id(0)
    seq_pos0 = (i * tm) % seq
    halo = jnp.where(seq_pos0 == 0, 0.0, uh_ref[...].astype(F32))
    ucat_ref[0:HALO, :] = halo
    ucat_ref[HALO:, :] = u_ref[...].astype(F32)

    pos = seq_pos0 + lax.broadcasted_iota(jnp.int32, (tm, 1), 0)
    parts = []
    for g, w in enumerate(POOL_WINDOWS):
        cols = slice(g * POOL_GROUP_WIDTH, (g + 1) * POOL_GROUP_WIDTH)
        cur = ucat_ref[HALO:HALO + tm, cols]
        s = cur
        for d in range(1, w):
            s = s + ucat_ref[HALO - d:HALO - d + tm, cols]
        count = jnp.minimum(pos + 1, w).astype(F32)
        pooled = s / count - cur
        pg = jnp.dot(pooled.astype(BF16), wpg_ref[g], preferred_element_type=F32)
        parts.append((pg * ps_ref[:, cols]).astype(BF16))
    y_pool = jnp.dot(jnp.concatenate(parts, axis=1), wpb_ref[...], preferred_element_type=F32)
    y_attn = jnp.dot(attn_ref[...], wab_ref[...], preferred_element_type=F32)
    mixed = (jax.nn.sigmoid(ga_ref[...].astype(F32)) * y_attn
             + jax.nn.sigmoid(gp_ref[...].astype(F32)) * y_pool)
    y = jnp.dot(mixed.astype(BF16), wout_ref[...], preferred_element_type=F32)
    o_ref[...] = x_ref[...] + _rmsnorm(y, gpost_ref[...])


def _mix(attn, proj, x2d, wab, wpg, pool_scale, wpb, wout, gpost, seq, *, tm=512):
    t, d = x2d.shape
    u_col = 3 * ATTN_WIDTH // POOL_WIDTH
    g_col = (3 * ATTN_WIDTH + POOL_WIDTH) // d
    const = lambda *shape: pl.BlockSpec(shape, lambda i: (0,) * len(shape), pipeline_mode=pl.Buffered(1))
    kern = functools.partial(_mix_kernel, tm=tm, seq=seq)
    return pl.pallas_call(
        kern,
        out_shape=jax.ShapeDtypeStruct((t, d), F32),
        grid=(t // tm,),
        in_specs=[
            pl.BlockSpec((tm, ATTN_WIDTH), lambda i: (i, 0)),
            pl.BlockSpec((tm, POOL_WIDTH), lambda i: (i, u_col)),
            pl.BlockSpec((HALO, POOL_WIDTH), lambda i: (jnp.maximum(i * (tm // HALO) - 1, 0), u_col)),
            pl.BlockSpec((tm, d), lambda i: (i, g_col)),
            pl.BlockSpec((tm, d), lambda i: (i, g_col + 1)),
            pl.BlockSpec((tm, d), lambda i: (i, 0)),
            const(ATTN_WIDTH, d),
            const(len(POOL_WINDOWS), POOL_GROUP_WIDTH, POOL_GROUP_WIDTH),
            const(1, POOL_WIDTH),
            const(POOL_WIDTH, d),
            const(d, d),
            const(1, d),
        ],
        out_specs=pl.BlockSpec((tm, d), lambda i: (i, 0)),
        scratch_shapes=[pltpu.VMEM((HALO + tm, POOL_WIDTH), F32)],
        compiler_params=_params("parallel"),
        name="mix",
    )(attn, proj, proj, proj, proj, x2d, wab, wpg, pool_scale, wpb, wout, gpost)


def _gelu_tanh(x):
    return 0.5 * x * (1.0 + jnp.tanh(0.7978845608028654 * (x + 0.044715 * (x * x * x))))


def _ffn_kernel(x_ref, xh_ref, gpre_ref, wg_ref, wv_ref, cwg_ref, cwv_ref, cbg_ref, cbv_ref, wd_ref,
                gpost_ref, o_ref, h_ref, up_ref, *, tm, seq):
    i = pl.program_id(0)
    j = pl.program_id(1)

    @pl.when(j == 0)
    def _():
        hh = _rmsnorm(xh_ref[...], gpre_ref[...])
        h_ref[0:HALO, :] = jnp.where((i * tm) % seq == 0, 0.0, hh).astype(BF16)
        h_ref[HALO:, :] = _rmsnorm(x_ref[...], gpre_ref[...]).astype(BF16)
        o_ref[...] = jnp.zeros_like(o_ref)

    h = h_ref[...]

    def conv(w_ref, cw_ref, cb_ref):
        up_ref[...] = jnp.dot(h, w_ref[...], preferred_element_type=F32)
        out = cb_ref[...]
        for tap in range(CONV_WIDTH):
            off = HALO - (CONV_WIDTH - 1 - tap)
            out = out + up_ref[off:off + tm, :] * cw_ref[tap:tap + 1, :]
        return out

    gate = conv(wg_ref, cwg_ref, cbg_ref)
    val = conv(wv_ref, cwv_ref, cbv_ref)
    act = (_gelu_tanh(gate) * val).astype(BF16)
    o_ref[...] += jnp.dot(act, wd_ref[...], preferred_element_type=F32)

    @pl.when(j == pl.num_programs(1) - 1)
    def _():
        o_ref[...] = x_ref[...] + _rmsnorm(o_ref[...], gpost_ref[...])


def _ffn(x2d, gpre, w_up, conv_w, conv_b, w_down, gpost, seq, *, tm=512, tf=512):
    t, d = x2d.shape
    d_ff = w_down.shape[0]
    nf = d_ff // tf
    kern = functools.partial(_ffn_kernel, tm=tm, seq=seq)
    return pl.pallas_call(
        kern,
        out_shape=jax.ShapeDtypeStruct((t, d), F32),
        grid=(t // tm, nf),
        in_specs=[
            pl.BlockSpec((tm, d), lambda i, j: (i, 0)),
            pl.BlockSpec((HALO, d), lambda i, j: (jnp.maximum(i * (tm // HALO) - 1, 0), 0)),
            pl.BlockSpec((1, d), lambda i, j: (0, 0)),
            pl.BlockSpec((d, tf), lambda i, j: (0, j)),
            pl.BlockSpec((d, tf), lambda i, j: (0, nf + j)),
            pl.BlockSpec((CONV_WIDTH, tf), lambda i, j: (0, j)),
            pl.BlockSpec((CONV_WIDTH, tf), lambda i, j: (0, nf + j)),
            pl.BlockSpec((1, tf), lambda i, j: (0, j)),
            pl.BlockSpec((1, tf), lambda i, j: (0, nf + j)),
            pl.BlockSpec((tf, d), lambda i, j: (j, 0)),
            pl.BlockSpec((1, d), lambda i, j: (0, 0)),
        ],
        out_specs=pl.BlockSpec((tm, d), lambda i, j: (i, 0)),
        scratch_shapes=[pltpu.VMEM((HALO + tm, d), BF16), pltpu.VMEM((HALO + tm, tf), F32)],
        compiler_params=_params("parallel", "arbitrary"),
        name="ffn",
    )(x2d, x2d, gpre, w_up, w_up, conv_w, conv_w, conv_b, conv_b, w_down, gpost)


def _ple_kernel(x_ref, p_ref, wple_ref, wgate_ref, gpost_ref, o_ref):
    x = x_ref[...]
    e = jnp.dot(p_ref[...].astype(BF16), wple_ref[...], preferred_element_type=F32)
    gate = jnp.dot(x.astype(BF16), wgate_ref[...], preferred_element_type=F32)
    o_ref[...] = x + _rmsnorm(jax.nn.sigmoid(gate) * e, gpost_ref[...])


def _ple(x2d, p2d, w_ple, w_gate, gpost, *, tm=512):
    t, d = x2d.shape
    pd = p2d.shape[1]
    return pl.pallas_call(
        _ple_kernel,
        out_shape=jax.ShapeDtypeStruct((t, d), F32),
        grid=(t // tm,),
        in_specs=[
            pl.BlockSpec((tm, d), lambda i: (i, 0)),
            pl.BlockSpec((tm, pd), lambda i: (i, 0)),
            pl.BlockSpec((pd, d), lambda i: (0, 0)),
            pl.BlockSpec((d, d), lambda i: (0, 0)),
            pl.BlockSpec((1, d), lambda i: (0, 0)),
        ],
        out_specs=pl.BlockSpec((tm, d), lambda i: (i, 0)),
        compiler_params=_params("parallel"),
        name="ple",
    )(x2d, p2d, w_ple, w_gate, gpost)


def kernel(x, p, norm_mix_pre, w_in, w_attn_branch, w_pool_group, pool_scale, w_pool_branch, w_out,
           norm_mix_post, norm_ffn_pre, w_up, conv_w, conv_b, w_down, norm_ffn_post, w_ple, w_ple_gate,
           norm_ple_post):
    batch, seq, d = x.shape
    depth = w_in.shape[0]
    x2d = x.reshape(batch * seq, d)
    row = lambda a: a.reshape(1, -1)
    for l in range(depth):
        proj = _in_proj(x2d, row(norm_mix_pre[l]), w_in[l].astype(BF16))
        attn = _attention(proj, batch, seq)
        x2d = _mix(attn, proj, x2d, w_attn_branch[l].astype(BF16), w_pool_group[l].astype(BF16),
                   row(pool_scale[l]), w_pool_branch[l].astype(BF16), w_out[l].astype(BF16),
                   row(norm_mix_post[l]), seq)
        x2d = _ffn(x2d, row(norm_ffn_pre[l]), w_up[l].astype(BF16), conv_w[l], row(conv_b[l]),
                   w_down[l].astype(BF16), row(norm_ffn_post[l]), seq)
        x2d = _ple(x2d, p[l].reshape(batch * seq, -1), w_ple[l].astype(BF16),
                   w_ple_gate[l].astype(BF16), row(norm_ple_post[l]))
    return x2d.reshape(batch, seq, d)
```

```python
import functools
import math

import jax
import jax.numpy as jnp
from jax import lax
from jax.experimental import pallas as pl
from jax.experimental.pallas import tpu as pltpu

N_ATTN_HEADS = 8
HEAD_DIM = 128
ATTN_WIDTH = N_ATTN_HEADS * HEAD_DIM
POOL_WINDOWS = (2, 4, 8, 16)
POOL_GROUP_WIDTH = 256
POOL_WIDTH = len(POOL_WINDOWS) * POOL_GROUP_WIDTH
CONV_WIDTH = 3
EPS = 1e-6

HALO = 16

Q_SCALE = HEAD_DIM ** -0.5 * math.log2(math.e)
F32_POW2_UNDERFLOW = 150.0

VMEM_LIMIT_BYTES = 56 * 1024 * 1024

F32 = jnp.float32
BF16 = jnp.bfloat16


def _rmsnorm(x, gain):
    return x * lax.rsqrt(jnp.mean(x * x, axis=-1, keepdims=True) + EPS) * gain


def _params(*semantics):
    return pltpu.CompilerParams(dimension_semantics=semantics, vmem_limit_bytes=VMEM_LIMIT_BYTES)


def _in_proj_kernel(x_ref, g_ref, w_ref, o_ref, h_ref, *, tn):
    j = pl.program_id(1)

    @pl.when(j == 0)
    def _():
        h_ref[...] = _rmsnorm(x_ref[...], g_ref[...]).astype(BF16)

    col_scale = jnp.where(j * tn < ATTN_WIDTH, Q_SCALE, 1.0)
    y = jnp.dot(h_ref[...], w_ref[...], preferred_element_type=F32)
    o_ref[...] = (y * col_scale).astype(o_ref.dtype)


def _in_proj(x2d, gain, w_bf16, *, tm=1024, tn=1024):
    t, d = x2d.shape
    n = w_bf16.shape[1]
    assert ATTN_WIDTH % tn == 0
    return pl.pallas_call(
        functools.partial(_in_proj_kernel, tn=tn),
        out_shape=jax.ShapeDtypeStruct((t, n), BF16),
        grid=(t // tm, n // tn),
        in_specs=[
            pl.BlockSpec((tm, d), lambda i, j: (i, 0)),
            pl.BlockSpec((1, d), lambda i, j: (0, 0)),
            pl.BlockSpec((d, tn), lambda i, j: (0, j)),
        ],
        out_specs=pl.BlockSpec((tm, tn), lambda i, j: (i, j)),
        scratch_shapes=[pltpu.VMEM((tm, d), BF16)],
        compiler_params=_params("parallel", "arbitrary"),
        name="in_proj",
    )(x2d, gain, w_bf16)


def _attn_kernel(q_ref, k_ref, v_ref, o_ref, acc_ref, carry_ref, *, tq, tk):
    qi = pl.program_id(2)
    q = q_ref[...]

    r = lax.broadcasted_iota(jnp.int32, (2 * tk, 2 * tk), 0)
    c = lax.broadcasted_iota(jnp.int32, (2 * tk, 2 * tk), 1)
    r = jnp.where(r >= tk, r - tk, r)
    sum_mat = jnp.where(c >= tk, 1.0, jnp.where(r > c, 1.0, 0.0)).astype(BF16)

    acc_ref[...] = jnp.zeros_like(acc_ref)
    carry_ref[...] = jnp.zeros_like(carry_ref)

    def block_sums(nl):
        hi = nl.astype(BF16)
        lo = (nl - hi.astype(F32)).astype(BF16)
        return jnp.dot(jnp.concatenate([hi, lo], axis=1), sum_mat, preferred_element_type=F32)

    def pair(kb, masked):
        ks = pl.multiple_of((kb - 1) * tk, tk)
        k = k_ref[pl.ds(ks, 2 * tk), :]
        v = v_ref[pl.ds(ks, 2 * tk), :]
        z = lax.dot_general(q, k, (((1,), (1,)), ((), ())), preferred_element_type=F32)
        nl = jnp.maximum(z, 0.0) + jnp.log2(1.0 + jnp.exp2(-jnp.abs(z)))
        w = z - nl
        if masked:
            t_pos = qi * tq + lax.broadcasted_iota(jnp.int32, (tq, 2 * tk), 0)
            s_pos = ks + lax.broadcasted_iota(jnp.int32, (tq, 2 * tk), 1)
            causal = s_pos < t_pos
            nl = jnp.where(causal, nl, 0.0)
        sums_hi = block_sums(nl[:, tk:])
        sums_lo = block_sums(nl[:, :tk])
        carry = carry_ref[...]
        carry_mid = carry + sums_hi[:, tk:]
        a = jnp.exp2(w - jnp.concatenate([sums_lo[:, :tk] + carry_mid, sums_hi[:, :tk] + carry], axis=1))
        if masked:
            a = jnp.where(causal, a, 0.0)
        acc_ref[...] += jnp.dot(a.astype(BF16), v, preferred_element_type=F32)
        carry_ref[...] = carry_mid + sums_lo[:, tk:]

    n_diag = tq // tk
    assert n_diag % 2 == 0
    for d in range(0, n_diag, 2):
        pair((qi + 1) * n_diag - 1 - d, masked=True)

    def cond(state):
        kb, live = state
        return jnp.logical_and(kb >= 1, live > 0)

    def body(state):
        kb, _ = state
        pair(kb, masked=False)
        live = (jnp.min(carry_ref[...]) <= F32_POW2_UNDERFLOW).astype(jnp.int32)
        return kb - 2, live

    lax.while_loop(cond, body, (qi * n_diag - 1, jnp.int32(1)))
    o_ref[...] = acc_ref[...].astype(o_ref.dtype)


def _attention(proj, batch, seq, *, tq=256, tk=128):
    t = proj.shape[0]
    nq = seq // tq
    kern = functools.partial(_attn_kernel, tq=tq, tk=tk)
    return pl.pallas_call(
        kern,
        out_shape=jax.ShapeDtypeStruct((t, ATTN_WIDTH), BF16),
        grid=(batch, N_ATTN_HEADS, nq),
        in_specs=[
            pl.BlockSpec((tq, HEAD_DIM), lambda b, h, i: (b * nq + i, h)),
            pl.BlockSpec((seq, HEAD_DIM), lambda b, h, i: (b, N_ATTN_HEADS + h)),
            pl.BlockSpec((seq, HEAD_DIM), lambda b, h, i: (b, 2 * N_ATTN_HEADS + h)),
        ],
        out_specs=pl.BlockSpec((tq, HEAD_DIM), lambda b, h, i: (b * nq + i, h)),
        scratch_shapes=[pltpu.VMEM((tq, HEAD_DIM), F32), pltpu.VMEM((tq, tk), F32)],
        compiler_params=_params("parallel", "parallel", "arbitrary"),
        name="attn",
    )(proj, proj, proj)


def _mix_kernel(attn_ref, u_ref, uh_ref, ga_ref, gp_ref, x_ref, wab_ref, wpg_ref, ps_ref, wpb_ref,
                wout_ref, gpost_ref, o_ref, ucat_ref, *, tm, seq):
    i = pl.program_id(0)
    seq_pos0 = (i * tm) % seq
    halo = jnp.where(seq_pos0 == 0, 0.0, uh_ref[...].astype(F32))
    ucat_ref[0:HALO, :] = halo
    ucat_ref[HALO:, :] = u_ref[...].astype(F32)

    pos = seq_pos0 + lax.broadcasted_iota(jnp.int32, (tm, 1), 0)
    parts = []
    for g, w in enumerate(POOL_WINDOWS):
        cols = slice(g * POOL_GROUP_WIDTH, (g + 1) * POOL_GROUP_WIDTH)
        cur = ucat_ref[HALO:HALO + tm, cols]
        s = cur
        for d in range(1, w):
            s = s + ucat_ref[HALO - d:HALO - d + tm, cols]
        count = jnp.minimum(pos + 1, w).astype(F32)
        pooled = s / count - cur
        pg = jnp.dot(pooled.astype(BF16), wpg_ref[g], preferred_element_type=F32)
        parts.append((pg * ps_ref[:, cols]).astype(BF16))
    y_pool = jnp.dot(jnp.concatenate(parts, axis=1), wpb_ref[...], preferred_element_type=F32)
    y_attn = jnp.dot(attn_ref[...], wab_ref[...], preferred_element_type=F32)
    mixed = (jax.nn.sigmoid(ga_ref[...].astype(F32)) * y_attn
             + jax.nn.sigmoid(gp_ref[...].astype(F32)) * y_pool)
    y = jnp.dot(mixed.astype(BF16), wout_ref[...], preferred_element_type=F32)
    o_ref[...] = x_ref[...] + _rmsnorm(y, gpost_ref[...])


def _mix(attn, proj, x2d, wab, wpg, pool_scale, wpb, wout, gpost, seq, *, tm=512):
    t, d = x2d.shape
    u_col = 3 * ATTN_WIDTH // POOL_WIDTH
    g_col = (3 * ATTN_WIDTH + POOL_WIDTH) // d
    const = lambda *shape: pl.BlockSpec(shape, lambda i: (0,) * len(shape), pipeline_mode=pl.Buffered(1))
    kern = functools.partial(_mix_kernel, tm=tm, seq=seq)
    return pl.pallas_call(
        kern,
        out_shape=jax.ShapeDtypeStruct((t, d), F32),
        grid=(t // tm,),
        in_specs=[
            pl.BlockSpec((tm, ATTN_WIDTH), lambda i: (i, 0)),
            pl.BlockSpec((tm, POOL_WIDTH), lambda i: (i, u_col)),
            pl.BlockSpec((HALO, POOL_WIDTH), lambda i: (jnp.maximum(i * (tm // HALO) - 1, 0), u_col)),
            pl.BlockSpec((tm, d), lambda i: (i, g_col)),
            pl.BlockSpec((tm, d), lambda i: (i, g_col + 1)),
            pl.BlockSpec((tm, d), lambda i: (i, 0)),
            const(ATTN_WIDTH, d),
            const(len(POOL_WINDOWS), POOL_GROUP_WIDTH, POOL_GROUP_WIDTH),
            const(1, POOL_WIDTH),
            const(POOL_WIDTH, d),
            const(d, d),
            const(1, d),
        ],
        out_specs=pl.BlockSpec((tm, d), lambda i: (i, 0)),
        scratch_shapes=[pltpu.VMEM((HALO + tm, POOL_WIDTH), F32)],
        compiler_params=_params("parallel"),
        name="mix",
    )(attn, proj, proj, proj, proj, x2d, wab, wpg, pool_scale, wpb, wout, gpost)


def _gelu_tanh(x):
    return 0.5 * x * (1.0 + jnp.tanh(0.7978845608028654 * (x + 0.044715 * (x * x * x))))


def _ffn_kernel(x_ref, xh_ref, gpre_ref, wg_ref, wv_ref, cwg_ref, cwv_ref, cbg_ref, cbv_ref, wd_ref,
                gpost_ref, o_ref, h_ref, up_ref, *, tm, seq):
    i = pl.program_id(0)
    j = pl.program_id(1)

    @pl.when(j == 0)
    def _():
        hh = _rmsnorm(xh_ref[...], gpre_ref[...])
        h_ref[0:HALO, :] = jnp.where((i * tm) % seq == 0, 0.0, hh).astype(BF16)
        h_ref[HALO:, :] = _rmsnorm(x_ref[...], gpre_ref[...]).astype(BF16)
        o_ref[...] = jnp.zeros_like(o_ref)

    h = h_ref[...]

    def conv(w_ref, cw_ref, cb_ref):
        up_ref[...] = jnp.dot(h, w_ref[...], preferred_element_type=F32)
        out = cb_ref[...]
        for tap in range(CONV_WIDTH):
            off = HALO - (CONV_WIDTH - 1 - tap)
            out = out + up_ref[off:off + tm, :] * cw_ref[tap:tap + 1, :]
        return out

    gate = conv(wg_ref, cwg_ref, cbg_ref)
    val = conv(wv_ref, cwv_ref, cbv_ref)
    act = (_gelu_tanh(gate) * val).astype(BF16)
    o_ref[...] += jnp.dot(act, wd_ref[...], preferred_element_type=F32)

    @pl.when(j == pl.num_programs(1) - 1)
    def _():
        o_ref[...] = x_ref[...] + _rmsnorm(o_ref[...], gpost_ref[...])


def _ffn(x2d, gpre, w_up, conv_w, conv_b, w_down, gpost, seq, *, tm=512, tf=512):
    t, d = x2d.shape
    d_ff = w_down.shape[0]
    nf = d_ff // tf
    kern = functools.partial(_ffn_kernel, tm=tm, seq=seq)
    return pl.pallas_call(
        kern,
        out_shape=jax.ShapeDtypeStruct((t, d), F32),
        grid=(t // tm, nf),
        in_specs=[
            pl.BlockSpec((tm, d), lambda i, j: (i, 0)),
            pl.BlockSpec((HALO, d), lambda i, j: (jnp.maximum(i * (tm // HALO) - 1, 0), 0)),
            pl.BlockSpec((1, d), lambda i, j: (0, 0)),
            pl.BlockSpec((d, tf), lambda i, j: (0, j)),
            pl.BlockSpec((d, tf), lambda i, j: (0, nf + j)),
            pl.BlockSpec((CONV_WIDTH, tf), lambda i, j: (0, j)),
            pl.BlockSpec((CONV_WIDTH, tf), lambda i, j: (0, nf + j)),
            pl.BlockSpec((1, tf), lambda i, j: (0, j)),
            pl.BlockSpec((1, tf), lambda i, j: (0, nf + j)),
            pl.BlockSpec((tf, d), lambda i, j: (j, 0)),
            pl.BlockSpec((1, d), lambda i, j: (0, 0)),
        ],
        out_specs=pl.BlockSpec((tm, d), lambda i, j: (i, 0)),
        scratch_shapes=[pltpu.VMEM((HALO + tm, d), BF16), pltpu.VMEM((HALO + tm, tf), F32)],
        compiler_params=_params("parallel", "arbitrary"),
        name="ffn",
    )(x2d, x2d, gpre, w_up, w_up, conv_w, conv_w, conv_b, conv_b, w_down, gpost)


def _ple_kernel(x_ref, p_ref, wple_ref, wgate_ref, gpost_ref, o_ref):
    x = x_ref[...]
    e = jnp.dot(p_ref[...].astype(BF16), wple_ref[...], preferred_element_type=F32)
    gate = jnp.dot(x.astype(BF16), wgate_ref[...], preferred_element_type=F32)
    o_ref[...] = x + _rmsnorm(jax.nn.sigmoid(gate) * e, gpost_ref[...])


def _ple(x2d, p2d, w_ple, w_gate, gpost, *, tm=512):
    t, d = x2d.shape
    pd = p2d.shape[1]
    return pl.pallas_call(
        _ple_kernel,
        out_shape=jax.ShapeDtypeStruct((t, d), F32),
        grid=(t // tm,),
        in_specs=[
            pl.BlockSpec((tm, d), lambda i: (i, 0)),
            pl.BlockSpec((tm, pd), lambda i: (i, 0)),
            pl.BlockSpec((pd, d), lambda i: (0, 0)),
            pl.BlockSpec((d, d), lambda i: (0, 0)),
            pl.BlockSpec((1, d), lambda i: (0, 0)),
        ],
        out_specs=pl.BlockSpec((tm, d), lambda i: (i, 0)),
        compiler_params=_params("parallel"),
        name="ple",
    )(x2d, p2d, w_ple, w_gate, gpost)


def kernel(x, p, norm_mix_pre, w_in, w_attn_branch, w_pool_group, pool_scale, w_pool_branch, w_out,
           norm_mix_post, norm_ffn_pre, w_up, conv_w, conv_b, w_down, norm_ffn_post, w_ple, w_ple_gate,
           norm_ple_post):
    batch, seq, d = x.shape
    depth = w_in.shape[0]
    x2d = x.reshape(batch * seq, d)
    row = lambda a: a.reshape(1, -1)
    for l in range(depth):
        proj = _in_proj(x2d, row(norm_mix_pre[l]), w_in[l].astype(BF16))
        attn = _attention(proj, batch, seq)
        x2d = _mix(attn, proj, x2d, w_attn_branch[l].astype(BF16), w_pool_group[l].astype(BF16),
                   row(pool_scale[l]), w_pool_branch[l].astype(BF16), w_out[l].astype(BF16),
                   row(norm_mix_post[l]), seq)
        x2d = _ffn(x2d, row(norm_ffn_pre[l]), w_up[l].astype(BF16), conv_w[l], row(conv_b[l]),
                   w_down[l].astype(BF16), row(norm_ffn_post[l]), seq)
        x2d = _ple(x2d, p[l].reshape(batch * seq, -1), w_ple[l].astype(BF16),
                   w_ple_gate[l].astype(BF16), row(norm_ple_post[l]))
    return x2d.reshape(batch, seq, d)
```

```python
import functools
import math

import jax
import jax.numpy as jnp
from jax import lax
from jax.experimental import pallas as pl
from jax.experimental.pallas import tpu as pltpu

N_ATTN_HEADS = 8
HEAD_DIM = 128
ATTN_WIDTH = N_ATTN_HEADS * HEAD_DIM
POOL_WINDOWS = (2, 4, 8, 16)
POOL_GROUP_WIDTH = 256
POOL_WIDTH = len(POOL_WINDOWS) * POOL_GROUP_WIDTH
CONV_WIDTH = 3
EPS = 1e-6

HALO = 16

Q_SCALE = HEAD_DIM ** -0.5 * math.log2(math.e)
F32_POW2_UNDERFLOW = 150.0

VMEM_LIMIT_BYTES = 56 * 1024 * 1024

F32 = jnp.float32
BF16 = jnp.bfloat16


def _rmsnorm(x, gain):
    return x * lax.rsqrt(jnp.mean(x * x, axis=-1, keepdims=True) + EPS) * gain


def _params(*semantics):
    return pltpu.CompilerParams(dimension_semantics=semantics, vmem_limit_bytes=VMEM_LIMIT_BYTES)


def _in_proj_kernel(x_ref, g_ref, w_ref, o_ref, h_ref, *, tn):
    j = pl.program_id(1)

    @pl.when(j == 0)
    def _():
        h_ref[...] = _rmsnorm(x_ref[...], g_ref[...]).astype(BF16)

    col_scale = jnp.where(j * tn < ATTN_WIDTH, Q_SCALE, 1.0)
    y = jnp.dot(h_ref[...], w_ref[...], preferred_element_type=F32)
    o_ref[...] = (y * col_scale).astype(o_ref.dtype)


def _in_proj(x2d, gain, w_bf16, *, tm=1024, tn=1024):
    t, d = x2d.shape
    n = w_bf16.shape[1]
    assert ATTN_WIDTH % tn == 0
    return pl.pallas_call(
        functools.partial(_in_proj_kernel, tn=tn),
        out_shape=jax.ShapeDtypeStruct((t, n), BF16),
        grid=(t // tm, n // tn),
        in_specs=[
            pl.BlockSpec((tm, d), lambda i, j: (i, 0)),
            pl.BlockSpec((1, d), lambda i, j: (0, 0)),
            pl.BlockSpec((d, tn), lambda i, j: (0, j)),
        ],
        out_specs=pl.BlockSpec((tm, tn), lambda i, j: (i, j)),
        scratch_shapes=[pltpu.VMEM((tm, d), BF16)],
        compiler_params=_params("parallel", "arbitrary"),
        name="in_proj",
    )(x2d, gain, w_bf16)


def _attn_kernel(q_ref, k_ref, v_ref, o_ref, acc_ref, carry_ref, *, tq, tk, heads):
    qi = pl.program_id(2)

    r = lax.broadcasted_iota(jnp.int32, (2 * tk, 2 * tk), 0)
    c = lax.broadcasted_iota(jnp.int32, (2 * tk, 2 * tk), 1)
    r = jnp.where(r >= tk, r - tk, r)
    sum_mat = jnp.where(c >= tk, 1.0, jnp.where(r > c, 1.0, 0.0)).astype(BF16)

    acc_ref[...] = jnp.zeros_like(acc_ref)
    carry_ref[...] = jnp.zeros_like(carry_ref)

    def block_sums(nl):
        hi = nl.astype(BF16)
        lo = (nl - hi.astype(F32)).astype(BF16)
        return jnp.dot(jnp.concatenate([hi, lo], axis=1), sum_mat, preferred_element_type=F32)

    def pair(kb, masked):
        ks = pl.multiple_of((kb - 1) * tk, tk)
        if masked:
            t_pos = qi * tq + lax.broadcasted_iota(jnp.int32, (tq, 2 * tk), 0)
            s_pos = ks + lax.broadcasted_iota(jnp.int32, (tq, 2 * tk), 1)
            causal = s_pos < t_pos
        for h in range(heads):
            dcol = slice(h * HEAD_DIM, (h + 1) * HEAD_DIM)
            ccol = slice(h * tk, (h + 1) * tk)
            k = k_ref[pl.ds(ks, 2 * tk), dcol]
            v = v_ref[pl.ds(ks, 2 * tk), dcol]
            z = lax.dot_general(q_ref[:, dcol], k, (((1,), (1,)), ((), ())), preferred_element_type=F32)
            neg_abs = lax.bitcast_convert_type(
                lax.bitcast_convert_type(z, jnp.uint32) | jnp.uint32(0x80000000), F32)
            nl = jnp.maximum(z, 0.0) + jnp.log2(1.0 + jnp.exp2(neg_abs))
            w = z - nl
            if masked:
                nl = jnp.where(causal, nl, 0.0)
            sums_hi = block_sums(nl[:, tk:])
            sums_lo = block_sums(nl[:, :tk])
            carry = carry_ref[:, ccol]
            carry_mid = carry + sums_hi[:, tk:]
            a = jnp.exp2(w - jnp.concatenate([sums_lo[:, :tk] + carry_mid, sums_hi[:, :tk] + carry], axis=1))
            if masked:
                a = jnp.where(causal, a, 0.0)
            acc_ref[:, dcol] += jnp.dot(a.astype(BF16), v, preferred_element_type=F32)
            carry_ref[:, ccol] = carry_mid + sums_lo[:, tk:]

    n_diag = tq // tk
    assert n_diag % 2 == 0
    for d in range(0, n_diag, 2):
        pair((qi + 1) * n_diag - 1 - d, masked=True)

    def cond(state):
        kb, live = state
        return jnp.logical_and(kb >= 1, live > 0)

    def body(state):
        kb, _ = state
        pair(kb, masked=False)
        live = (jnp.min(carry_ref[...]) <= F32_POW2_UNDERFLOW).astype(jnp.int32)
        return kb - 2, live

    lax.while_loop(cond, body, (qi * n_diag - 1, jnp.int32(1)))
    o_ref[...] = acc_ref[...].astype(o_ref.dtype)


def _attention(proj, batch, seq, *, tq=256, tk=128, heads=4):
    t = proj.shape[0]
    nq = seq // tq
    hg = N_ATTN_HEADS // heads
    width = heads * HEAD_DIM
    kern = functools.partial(_attn_kernel, tq=tq, tk=tk, heads=heads)
    return pl.pallas_call(
        kern,
        out_shape=jax.ShapeDtypeStruct((t, ATTN_WIDTH), BF16),
        grid=(batch, hg, nq),
        in_specs=[
            pl.BlockSpec((tq, width), lambda b, g, i: (b * nq + i, g)),
            pl.BlockSpec((seq, width), lambda b, g, i: (b, hg + g)),
            pl.BlockSpec((seq, width), lambda b, g, i: (b, 2 * hg + g)),
        ],
        out_specs=pl.BlockSpec((tq, width), lambda b, g, i: (b * nq + i, g)),
        scratch_shapes=[pltpu.VMEM((tq, width), F32), pltpu.VMEM((tq, heads * tk), F32)],
        compiler_params=_params("parallel", "parallel", "arbitrary"),
        name="attn",
    )(proj, proj, proj)


def _mix_kernel(attn_ref, u_ref, uh_ref, ga_ref, gp_ref, x_ref, wab_ref, wpg_ref, ps_ref, wpb_ref,
                wout_ref, gpost_ref, o_ref, ucat_ref, *, tm, seq):
    i = pl.program_id(0)
    seq_pos0 = (i * tm) % seq
    halo = jnp.where(seq_pos0 == 0, 0.0, uh_ref[...].astype(F32))
    ucat_ref[0:HALO, :] = halo
    ucat_ref[HALO:, :] = u_ref[...].astype(F32)

    pos = seq_pos0 + lax.broadcasted_iota(jnp.int32, (tm, 1), 0)
    parts = []
    for g, w in enumerate(POOL_WINDOWS):
        cols = slice(g * POOL_GROUP_WIDTH, (g + 1) * POOL_GROUP_WIDTH)
        cur = ucat_ref[HALO:HALO + tm, cols]
        s = cur
        for d in range(1, w):
            s = s + ucat_ref[HALO - d:HALO - d + tm, cols]
        count = jnp.minimum(pos + 1, w).astype(F32)
        pooled = s / count - cur
        pg = jnp.dot(pooled.astype(BF16), wpg_ref[g], preferred_element_type=F32)
        parts.append((pg * ps_ref[:, cols]).astype(BF16))
    y_pool = jnp.dot(jnp.concatenate(parts, axis=1), wpb_ref[...], preferred_element_type=F32)
    y_attn = jnp.dot(attn_ref[...], wab_ref[...], preferred_element_type=F32)
    mixed = (jax.nn.sigmoid(ga_ref[...].astype(F32)) * y_attn
             + jax.nn.sigmoid(gp_ref[...].astype(F32)) * y_pool)
    y = jnp.dot(mixed.astype(BF16), wout_ref[...], preferred_element_type=F32)
    o_ref[...] = x_ref[...] + _rmsnorm(y, gpost_ref[...])


def _mix(attn, proj, x2d, wab, wpg, pool_scale, wpb, wout, gpost, seq, *, tm=512):
    t, d = x2d.shape
    u_col = 3 * ATTN_WIDTH // POOL_WIDTH
    g_col = (3 * ATTN_WIDTH + POOL_WIDTH) // d
    const = lambda *shape: pl.BlockSpec(shape, lambda i: (0,) * len(shape), pipeline_mode=pl.Buffered(1))
    kern = functools.partial(_mix_kernel, tm=tm, seq=seq)
    return pl.pallas_call(
        kern,
        out_shape=jax.ShapeDtypeStruct((t, d), F32),
        grid=(t // tm,),
        in_specs=[
            pl.BlockSpec((tm, ATTN_WIDTH), lambda i: (i, 0)),
            pl.BlockSpec((tm, POOL_WIDTH), lambda i: (i, u_col)),
            pl.BlockSpec((HALO, POOL_WIDTH), lambda i: (jnp.maximum(i * (tm // HALO) - 1, 0), u_col)),
            pl.BlockSpec((tm, d), lambda i: (i, g_col)),
            pl.BlockSpec((tm, d), lambda i: (i, g_col + 1)),
            pl.BlockSpec((tm, d), lambda i: (i, 0)),
            const(ATTN_WIDTH, d),
            const(len(POOL_WINDOWS), POOL_GROUP_WIDTH, POOL_GROUP_WIDTH),
            const(1, POOL_WIDTH),
            const(POOL_WIDTH, d),
            const(d, d),
            const(1, d),
        ],
        out_specs=pl.BlockSpec((tm, d), lambda i: (i, 0)),
        scratch_shapes=[pltpu.VMEM((HALO + tm, POOL_WIDTH), F32)],
        compiler_params=_params("parallel"),
        name="mix",
    )(attn, proj, proj, proj, proj, x2d, wab, wpg, pool_scale, wpb, wout, gpost)


def _gelu_tanh(x):
    return 0.5 * x * (1.0 + jnp.tanh(0.7978845608028654 * (x + 0.044715 * (x * x * x))))


def _ffn_kernel(x_ref, xh_ref, gpre_ref, wg_ref, wv_ref, cwg_ref, cwv_ref, cbg_ref, cbv_ref, wd_ref,
                gpost_ref, o_ref, h_ref, up_ref, *, tm, seq):
    i = pl.program_id(0)
    j = pl.program_id(1)

    @pl.when(j == 0)
    def _():
        hh = _rmsnorm(xh_ref[...], gpre_ref[...])
        h_ref[0:HALO, :] = jnp.where((i * tm) % seq == 0, 0.0, hh).astype(BF16)
        h_ref[HALO:, :] = _rmsnorm(x_ref[...], gpre_ref[...]).astype(BF16)
        o_ref[...] = jnp.zeros_like(o_ref)

    h = h_ref[...]

    def conv(w_ref, cw_ref, cb_ref):
        up_ref[...] = jnp.dot(h, w_ref[...], preferred_element_type=F32)
        out = cb_ref[...]
        for tap in range(CONV_WIDTH):
            off = HALO - (CONV_WIDTH - 1 - tap)
            out = out + up_ref[off:off + tm, :] * cw_ref[tap:tap + 1, :]
        return out

    gate = conv(wg_ref, cwg_ref, cbg_ref)
    val = conv(wv_ref, cwv_ref, cbv_ref)
    act = (_gelu_tanh(gate) * val).astype(BF16)
    o_ref[...] += jnp.dot(act, wd_ref[...], preferred_element_type=F32)

    @pl.when(j == pl.num_programs(1) - 1)
    def _():
        o_ref[...] = x_ref[...] + _rmsnorm(o_ref[...], gpost_ref[...])


def _ffn(x2d, gpre, w_up, conv_w, conv_b, w_down, gpost, seq, *, tm=1024, tf=512):
    t, d = x2d.shape
    d_ff = w_down.shape[0]
    nf = d_ff // tf
    kern = functools.partial(_ffn_kernel, tm=tm, seq=seq)
    return pl.pallas_call(
        kern,
        out_shape=jax.ShapeDtypeStruct((t, d), F32),
        grid=(t // tm, nf),
        in_specs=[
            pl.BlockSpec((tm, d), lambda i, j: (i, 0), pipeline_mode=pl.Buffered(1)),
            pl.BlockSpec((HALO, d), lambda i, j: (jnp.maximum(i * (tm // HALO) - 1, 0), 0)),
            pl.BlockSpec((1, d), lambda i, j: (0, 0)),
            pl.BlockSpec((d, tf), lambda i, j: (0, j)),
            pl.BlockSpec((d, tf), lambda i, j: (0, nf + j)),
            pl.BlockSpec((CONV_WIDTH, tf), lambda i, j: (0, j)),
            pl.BlockSpec((CONV_WIDTH, tf), lambda i, j: (0, nf + j)),
            pl.BlockSpec((1, tf), lambda i, j: (0, j)),
            pl.BlockSpec((1, tf), lambda i, j: (0, nf + j)),
            pl.BlockSpec((tf, d), lambda i, j: (j, 0)),
            pl.BlockSpec((1, d), lambda i, j: (0, 0)),
        ],
        out_specs=pl.BlockSpec((tm, d), lambda i, j: (i, 0)),
        scratch_shapes=[pltpu.VMEM((HALO + tm, d), BF16), pltpu.VMEM((HALO + tm, tf), F32)],
        compiler_params=_params("parallel", "arbitrary"),
        name="ffn",
    )(x2d, x2d, gpre, w_up, w_up, conv_w, conv_w, conv_b, conv_b, w_down, gpost)


def _ple_kernel(x_ref, p_ref, wple_ref, wgate_ref, gpost_ref, o_ref):
    x = x_ref[...]
    e = jnp.dot(p_ref[...].astype(BF16), wple_ref[...], preferred_element_type=F32)
    gate = jnp.dot(x.astype(BF16), wgate_ref[...], preferred_element_type=F32)
    o_ref[...] = x + _rmsnorm(jax.nn.sigmoid(gate) * e, gpost_ref[...])


def _ple(x2d, p2d, w_ple, w_gate, gpost, *, tm=512):
    t, d = x2d.shape
    pd = p2d.shape[1]
    return pl.pallas_call(
        _ple_kernel,
        out_shape=jax.ShapeDtypeStruct((t, d), F32),
        grid=(t // tm,),
        in_specs=[
            pl.BlockSpec((tm, d), lambda i: (i, 0)),
            pl.BlockSpec((tm, pd), lambda i: (i, 0)),
            pl.BlockSpec((pd, d), lambda i: (0, 0)),
            pl.BlockSpec((d, d), lambda i: (0, 0)),
            pl.BlockSpec((1, d), lambda i: (0, 0)),
        ],
        out_specs=pl.BlockSpec((tm, d), lambda i: (i, 0)),
        compiler_params=_params("parallel"),
        name="ple",
    )(x2d, p2d, w_ple, w_gate, gpost)


def kernel(x, p, norm_mix_pre, w_in, w_attn_branch, w_pool_group, pool_scale, w_pool_branch, w_out,
           norm_mix_post, norm_ffn_pre, w_up, conv_w, conv_b, w_down, norm_ffn_post, w_ple, w_ple_gate,
           norm_ple_post):
    batch, seq, d = x.shape
    depth = w_in.shape[0]
    x2d = x.reshape(batch * seq, d)
    row = lambda a: a.reshape(1, -1)
    for l in range(depth):
        proj = _in_proj(x2d, row(norm_mix_pre[l]), w_in[l].astype(BF16))
        attn = _attention(proj, batch, seq)
        x2d = _mix(attn, proj, x2d, w_attn_branch[l].astype(BF16), w_pool_group[l].astype(BF16),
                   row(pool_scale[l]), w_pool_branch[l].astype(BF16), w_out[l].astype(BF16),
                   row(norm_mix_post[l]), seq)
        x2d = _ffn(x2d, row(norm_ffn_pre[l]), w_up[l].astype(BF16), conv_w[l], row(conv_b[l]),
                   w_down[l].astype(BF16), row(norm_ffn_post[l]), seq)
        x2d = _ple(x2d, p[l].reshape(batch * seq, -1), w_ple[l].astype(BF16),
                   w_ple_gate[l].astype(BF16), row(norm_ple_post[l]))
    return x2d.reshape(batch, seq, d)
```

```python
import functools
import math

import jax
import jax.numpy as jnp
from jax import lax
from jax.experimental import pallas as pl
from jax.experimental.pallas import tpu as pltpu

N_ATTN_HEADS = 8
HEAD_DIM = 128
ATTN_WIDTH = N_ATTN_HEADS * HEAD_DIM
POOL_WINDOWS = (2, 4, 8, 16)
POOL_GROUP_WIDTH = 256
POOL_WIDTH = len(POOL_WINDOWS) * POOL_GROUP_WIDTH
CONV_WIDTH = 3
EPS = 1e-6

HALO = 16

Q_SCALE = HEAD_DIM ** -0.5 * math.log2(math.e)
F32_POW2_UNDERFLOW = 150.0

VMEM_LIMIT_BYTES = 56 * 1024 * 1024

F32 = jnp.float32
BF16 = jnp.bfloat16


def _rmsnorm(x, gain):
    return x * lax.rsqrt(jnp.mean(x * x, axis=-1, keepdims=True) + EPS) * gain


def _params(*semantics):
    return pltpu.CompilerParams(dimension_semantics=semantics, vmem_limit_bytes=VMEM_LIMIT_BYTES)


def _in_proj_kernel(x_ref, g_ref, w_ref, o_ref, h_ref, *, tn):
    j = pl.program_id(1)

    @pl.when(j == 0)
    def _():
        h_ref[...] = _rmsnorm(x_ref[...], g_ref[...]).astype(BF16)

    col_scale = jnp.where(j * tn < ATTN_WIDTH, Q_SCALE, 1.0)
    y = jnp.dot(h_ref[...], w_ref[...], preferred_element_type=F32)
    o_ref[...] = (y * col_scale).astype(o_ref.dtype)


def _in_proj(x2d, gain, w_bf16, *, tm=1024, tn=1024):
    t, d = x2d.shape
    n = w_bf16.shape[1]
    assert ATTN_WIDTH % tn == 0
    return pl.pallas_call(
        functools.partial(_in_proj_kernel, tn=tn),
        out_shape=jax.ShapeDtypeStruct((t, n), BF16),
        grid=(t // tm, n // tn),
        in_specs=[
            pl.BlockSpec((tm, d), lambda i, j: (i, 0)),
            pl.BlockSpec((1, d), lambda i, j: (0, 0)),
            pl.BlockSpec((d, tn), lambda i, j: (0, j)),
        ],
        out_specs=pl.BlockSpec((tm, tn), lambda i, j: (i, j)),
        scratch_shapes=[pltpu.VMEM((tm, d), BF16)],
        compiler_params=_params("parallel", "arbitrary"),
        name="in_proj",
    )(x2d, gain, w_bf16)


def _attn_kernel(q_ref, k_ref, v_ref, o_ref, acc_ref, carry_ref, *, tq, tk, heads):
    qi = pl.program_id(2)

    r = lax.broadcasted_iota(jnp.int32, (2 * tk, 2 * tk), 0)
    c = lax.broadcasted_iota(jnp.int32, (2 * tk, 2 * tk), 1)
    r = jnp.where(r >= tk, r - tk, r)
    sum_mat = jnp.where(c >= tk, 1.0, jnp.where(r > c, 1.0, 0.0)).astype(BF16)

    acc_ref[...] = jnp.zeros_like(acc_ref)
    carry_ref[...] = jnp.zeros_like(carry_ref)

    def block_sums(nl):
        hi = nl.astype(BF16)
        lo = (nl - hi.astype(F32)).astype(BF16)
        return jnp.dot(jnp.concatenate([hi, lo], axis=1), sum_mat, preferred_element_type=F32)

    def pair(kb, masked):
        ks = pl.multiple_of((kb - 1) * tk, tk)
        if masked:
            t_pos = qi * tq + lax.broadcasted_iota(jnp.int32, (tq, 2 * tk), 0)
            s_pos = ks + lax.broadcasted_iota(jnp.int32, (tq, 2 * tk), 1)
            causal = s_pos < t_pos
        for h in range(heads):
            dcol = slice(h * HEAD_DIM, (h + 1) * HEAD_DIM)
            ccol = slice(h * tk, (h + 1) * tk)
            k = k_ref[pl.ds(ks, 2 * tk), dcol]
            v = v_ref[pl.ds(ks, 2 * tk), dcol]
            z = lax.dot_general(q_ref[:, dcol], k, (((1,), (1,)), ((), ())), preferred_element_type=F32)
            nl = jnp.maximum(z, 0.0) + jnp.log2(1.0 + jnp.exp2(-jnp.abs(z)))
            w = z - nl
            if masked:
                nl = jnp.where(causal, nl, 0.0)
            sums_hi = block_sums(nl[:, tk:])
            sums_lo = block_sums(nl[:, :tk])
            carry = carry_ref[:, ccol]
            carry_mid = carry + sums_hi[:, tk:]
            a = jnp.exp2(w - jnp.concatenate([sums_lo[:, :tk] + carry_mid, sums_hi[:, :tk] + carry], axis=1))
            if masked:
                a = jnp.where(causal, a, 0.0)
            acc_ref[:, dcol] += jnp.dot(a.astype(BF16), v, preferred_element_type=F32)
            carry_ref[:, ccol] = carry_mid + sums_lo[:, tk:]

    n_diag = tq // tk
    assert n_diag % 2 == 0
    for d in range(0, n_diag, 2):
        pair((qi + 1) * n_diag - 1 - d, masked=True)

    def cond(state):
        kb, live = state
        return jnp.logical_and(kb >= 1, live > 0)

    def body(state):
        kb, _ = state
        pair(kb, masked=False)
        live = (jnp.min(carry_ref[...]) <= F32_POW2_UNDERFLOW).astype(jnp.int32)
        return kb - 2, live

    lax.while_loop(cond, body, (qi * n_diag - 1, jnp.int32(1)))
    o_ref[...] = acc_ref[...].astype(o_ref.dtype)


def _attention(proj, batch, seq, *, tq=256, tk=128, heads=4):
    t = proj.shape[0]
    nq = seq // tq
    hg = N_ATTN_HEADS // heads
    width = heads * HEAD_DIM
    kern = functools.partial(_attn_kernel, tq=tq, tk=tk, heads=heads)
    return pl.pallas_call(
        kern,
        out_shape=jax.ShapeDtypeStruct((t, ATTN_WIDTH), BF16),
        grid=(batch, hg, nq),
        in_specs=[
            pl.BlockSpec((tq, width), lambda b, g, i: (b * nq + i, g)),
            pl.BlockSpec((seq, width), lambda b, g, i: (b, hg + g)),
            pl.BlockSpec((seq, width), lambda b, g, i: (b, 2 * hg + g)),
        ],
        out_specs=pl.BlockSpec((tq, width), lambda b, g, i: (b * nq + i, g)),
        scratch_shapes=[pltpu.VMEM((tq, width), F32), pltpu.VMEM((tq, heads * tk), F32)],
        compiler_params=_params("parallel", "parallel", "arbitrary"),
        name="attn",
    )(proj, proj, proj)


def _mix_kernel(attn_ref, u_ref, uh_ref, ga_ref, gp_ref, x_ref, wab_ref, wpg_ref, ps_ref, wpb_ref,
                wout_ref, gpost_ref, o_ref, *, tm, seq):
    i = pl.program_id(0)
    seq_pos0 = (i * tm) % seq
    halo = jnp.where(seq_pos0 == 0, 0.0, uh_ref[...].astype(F32))
    ucat = jnp.concatenate([halo, u_ref[...].astype(F32)], axis=0)

    pos = seq_pos0 + lax.broadcasted_iota(jnp.int32, (tm, 1), 0)
    parts = []
    y_attn_parts = []
    n_groups = len(POOL_WINDOWS)
    d = o_ref.shape[1]
    for g, w in enumerate(POOL_WINDOWS):
        acols = slice(g * d // n_groups, (g + 1) * d // n_groups)
        y_attn_parts.append(jnp.dot(attn_ref[...], wab_ref[:, acols], preferred_element_type=F32))
        cols = slice(g * POOL_GROUP_WIDTH, (g + 1) * POOL_GROUP_WIDTH)
        s = ucat[:, cols]
        k = 1
        while k < w:
            s = s + pltpu.roll(s, k, axis=0)
            k *= 2
        count = jnp.minimum(pos + 1, w).astype(F32)
        pooled = s[HALO:] / count - ucat[HALO:, cols]
        pg = jnp.dot(pooled.astype(BF16), wpg_ref[g], preferred_element_type=F32)
        parts.append((pg * ps_ref[:, cols]).astype(BF16))
    y_attn = jnp.concatenate(y_attn_parts, axis=1)
    y_pool = jnp.dot(jnp.concatenate(parts, axis=1), wpb_ref[...], preferred_element_type=F32)
    mixed = (jax.nn.sigmoid(ga_ref[...].astype(F32)) * y_attn
             + jax.nn.sigmoid(gp_ref[...].astype(F32)) * y_pool)
    y = jnp.dot(mixed.astype(BF16), wout_ref[...], preferred_element_type=F32)
    o_ref[...] = x_ref[...] + _rmsnorm(y, gpost_ref[...])


def _mix(attn, proj, x2d, wab, wpg, pool_scale, wpb, wout, gpost, seq, *, tm=512):
    t, d = x2d.shape
    assert all(w & (w - 1) == 0 and w <= HALO for w in POOL_WINDOWS)
    u_col = 3 * ATTN_WIDTH // POOL_WIDTH
    g_col = (3 * ATTN_WIDTH + POOL_WIDTH) // d
    const = lambda *shape: pl.BlockSpec(shape, lambda i: (0,) * len(shape), pipeline_mode=pl.Buffered(1))
    kern = functools.partial(_mix_kernel, tm=tm, seq=seq)
    return pl.pallas_call(
        kern,
        out_shape=jax.ShapeDtypeStruct((t, d), F32),
        grid=(t // tm,),
        in_specs=[
            pl.BlockSpec((tm, ATTN_WIDTH), lambda i: (i, 0)),
            pl.BlockSpec((tm, POOL_WIDTH), lambda i: (i, u_col)),
            pl.BlockSpec((HALO, POOL_WIDTH), lambda i: (jnp.maximum(i * (tm // HALO) - 1, 0), u_col)),
            pl.BlockSpec((tm, d), lambda i: (i, g_col)),
            pl.BlockSpec((tm, d), lambda i: (i, g_col + 1)),
            pl.BlockSpec((tm, d), lambda i: (i, 0)),
            const(ATTN_WIDTH, d),
            const(len(POOL_WINDOWS), POOL_GROUP_WIDTH, POOL_GROUP_WIDTH),
            const(1, POOL_WIDTH),
            const(POOL_WIDTH, d),
            const(d, d),
            const(1, d),
        ],
        out_specs=pl.BlockSpec((tm, d), lambda i: (i, 0)),
        compiler_params=_params("parallel"),
        name="mix",
    )(attn, proj, proj, proj, proj, x2d, wab, wpg, pool_scale, wpb, wout, gpost)


def _gelu_tanh(x):
    return 0.5 * x * (1.0 + jnp.tanh(0.7978845608028654 * (x + 0.044715 * (x * x * x))))


def _ffn_kernel(x_ref, xh_ref, gpre_ref, wg_ref, wv_ref, cwg_ref, cwv_ref, cbg_ref, cbv_ref, wd_ref,
                gpost_ref, o_ref, h_ref, up_ref, *, tm, seq, n_sub):
    i = pl.program_id(0)
    j = pl.program_id(1)

    @pl.when(j == 0)
    def _():
        hh = _rmsnorm(xh_ref[...], gpre_ref[...])
        h_ref[0:HALO, :] = jnp.where((i * tm) % seq == 0, 0.0, hh).astype(BF16)
        h_ref[HALO:, :] = _rmsnorm(x_ref[...], gpre_ref[...]).astype(BF16)
        o_ref[...] = jnp.zeros_like(o_ref)

    h = h_ref[...]

    def conv(slot, w_ref, cw_ref, cb_ref, cols):
        up_ref[slot] = jnp.dot(h, w_ref[:, cols], preferred_element_type=F32)
        out = cb_ref[:, cols]
        for tap in range(CONV_WIDTH):
            off = HALO - (CONV_WIDTH - 1 - tap)
            out = out + up_ref[slot, off:off + tm, :] * cw_ref[tap:tap + 1, cols]
        return out

    ts = wd_ref.shape[0] // n_sub
    for c in range(n_sub):
        cols = slice(c * ts, (c + 1) * ts)
        gate = conv(2 * c, wg_ref, cwg_ref, cbg_ref, cols)
        val = conv(2 * c + 1, wv_ref, cwv_ref, cbv_ref, cols)
        act = (_gelu_tanh(gate) * val).astype(BF16)
        o_ref[...] += jnp.dot(act, wd_ref[cols, :], preferred_element_type=F32)

    @pl.when(j == pl.num_programs(1) - 1)
    def _():
        o_ref[...] = x_ref[...] + _rmsnorm(o_ref[...], gpost_ref[...])


def _ffn(x2d, gpre, w_up, conv_w, conv_b, w_down, gpost, seq, *, tm=512, tf=512, n_sub=1):
    t, d = x2d.shape
    d_ff = w_down.shape[0]
    nf = d_ff // tf
    kern = functools.partial(_ffn_kernel, tm=tm, seq=seq, n_sub=n_sub)
    return pl.pallas_call(
        kern,
        out_shape=jax.ShapeDtypeStruct((t, d), F32),
        grid=(t // tm, nf),
        in_specs=[
            pl.BlockSpec((tm, d), lambda i, j: (i, 0)),
            pl.BlockSpec((HALO, d), lambda i, j: (jnp.maximum(i * (tm // HALO) - 1, 0), 0)),
            pl.BlockSpec((1, d), lambda i, j: (0, 0)),
            pl.BlockSpec((d, tf), lambda i, j: (0, j)),
            pl.BlockSpec((d, tf), lambda i, j: (0, nf + j)),
            pl.BlockSpec((CONV_WIDTH, tf), lambda i, j: (0, j)),
            pl.BlockSpec((CONV_WIDTH, tf), lambda i, j: (0, nf + j)),
            pl.BlockSpec((1, tf), lambda i, j: (0, j)),
            pl.BlockSpec((1, tf), lambda i, j: (0, nf + j)),
            pl.BlockSpec((tf, d), lambda i, j: (j, 0)),
            pl.BlockSpec((1, d), lambda i, j: (0, 0)),
        ],
        out_specs=pl.BlockSpec((tm, d), lambda i, j: (i, 0)),
        scratch_shapes=[pltpu.VMEM((HALO + tm, d), BF16),
                        pltpu.VMEM((2 * n_sub, HALO + tm, tf // n_sub), F32)],
        compiler_params=_params("parallel", "arbitrary"),
        name="ffn",
    )(x2d, x2d, gpre, w_up, w_up, conv_w, conv_w, conv_b, conv_b, w_down, gpost)


def _ple_kernel(x_ref, p_ref, wple_ref, wgate_ref, gpost_ref, o_ref):
    x = x_ref[...]
    e = jnp.dot(p_ref[...].astype(BF16), wple_ref[...], preferred_element_type=F32)
    gate = jnp.dot(x.astype(BF16), wgate_ref[...], preferred_element_type=F32)
    o_ref[...] = x + _rmsnorm(jax.nn.sigmoid(gate) * e, gpost_ref[...])


def _ple(x2d, p2d, w_ple, w_gate, gpost, *, tm=512):
    t, d = x2d.shape
    pd = p2d.shape[1]
    return pl.pallas_call(
        _ple_kernel,
        out_shape=jax.ShapeDtypeStruct((t, d), F32),
        grid=(t // tm,),
        in_specs=[
            pl.BlockSpec((tm, d), lambda i: (i, 0)),
            pl.BlockSpec((tm, pd), lambda i: (i, 0)),
            pl.BlockSpec((pd, d), lambda i: (0, 0)),
            pl.BlockSpec((d, d), lambda i: (0, 0)),
            pl.BlockSpec((1, d), lambda i: (0, 0)),
        ],
        out_specs=pl.BlockSpec((tm, d), lambda i: (i, 0)),
        compiler_params=_params("parallel"),
        name="ple",
    )(x2d, p2d, w_ple, w_gate, gpost)


def kernel(x, p, norm_mix_pre, w_in, w_attn_branch, w_pool_group, pool_scale, w_pool_branch, w_out,
           norm_mix_post, norm_ffn_pre, w_up, conv_w, conv_b, w_down, norm_ffn_post, w_ple, w_ple_gate,
           norm_ple_post):
    batch, seq, d = x.shape
    depth = w_in.shape[0]
    x2d = x.reshape(batch * seq, d)
    row = lambda a: a.reshape(1, -1)
    for l in range(depth):
        proj = _in_proj(x2d, row(norm_mix_pre[l]), w_in[l].astype(BF16))
        attn = _attention(proj, batch, seq)
        x2d = _mix(attn, proj, x2d, w_attn_branch[l].astype(BF16), w_pool_group[l].astype(BF16),
                   row(pool_scale[l]), w_pool_branch[l].astype(BF16), w_out[l].astype(BF16),
                   row(norm_mix_post[l]), seq)
        x2d = _ffn(x2d, row(norm_ffn_pre[l]), w_up[l].astype(BF16), conv_w[l], row(conv_b[l]),
                   w_down[l].astype(BF16), row(norm_ffn_post[l]), seq)
        x2d = _ple(x2d, p[l].reshape(batch * seq, -1), w_ple[l].astype(BF16),
                   w_ple_gate[l].astype(BF16), row(norm_ple_post[l]))
    return x2d.reshape(batch, seq, d)
```

```python
import functools
import math

import jax
import jax.numpy as jnp
from jax import lax
from jax.experimental import pallas as pl
from jax.experimental.pallas import tpu as pltpu

N_ATTN_HEADS = 8
HEAD_DIM = 128
ATTN_WIDTH = N_ATTN_HEADS * HEAD_DIM
POOL_WINDOWS = (2, 4, 8, 16)
POOL_GROUP_WIDTH = 256
POOL_WIDTH = len(POOL_WINDOWS) * POOL_GROUP_WIDTH
CONV_WIDTH = 3
EPS = 1e-6

HALO = 16

Q_SCALE = HEAD_DIM ** -0.5 * math.log2(math.e)
F32_POW2_UNDERFLOW = 150.0

VMEM_LIMIT_BYTES = 56 * 1024 * 1024

F32 = jnp.float32
BF16 = jnp.bfloat16


def _rmsnorm(x, gain):
    return x * lax.rsqrt(jnp.mean(x * x, axis=-1, keepdims=True) + EPS) * gain


def _params(*semantics):
    return pltpu.CompilerParams(dimension_semantics=semantics, vmem_limit_bytes=VMEM_LIMIT_BYTES)


def _in_proj_kernel(x_ref, g_ref, w_ref, o_ref, h_ref, *, tn):
    j = pl.program_id(1)

    @pl.when(j == 0)
    def _():
        h_ref[...] = _rmsnorm(x_ref[...], g_ref[...]).astype(BF16)

    col = j * tn + lax.broadcasted_iota(jnp.int32, (1, tn), 1)
    col_scale = jnp.where(col < ATTN_WIDTH, Q_SCALE, 1.0)
    y = jnp.dot(h_ref[...], w_ref[...], preferred_element_type=F32)
    o_ref[...] = (y * col_scale).astype(o_ref.dtype)


def _in_proj(x2d, gain, w_bf16, *, tm=1024, tn=2048):
    t, d = x2d.shape
    n = w_bf16.shape[1]
    return pl.pallas_call(
        functools.partial(_in_proj_kernel, tn=tn),
        out_shape=jax.ShapeDtypeStruct((t, n), BF16),
        grid=(t // tm, n // tn),
        in_specs=[
            pl.BlockSpec((tm, d), lambda i, j: (i, 0)),
            pl.BlockSpec((1, d), lambda i, j: (0, 0)),
            pl.BlockSpec((d, tn), lambda i, j: (0, j)),
        ],
        out_specs=pl.BlockSpec((tm, tn), lambda i, j: (i, j)),
        scratch_shapes=[pltpu.VMEM((tm, d), BF16)],
        compiler_params=_params("parallel", "arbitrary"),
        name="in_proj",
    )(x2d, gain, w_bf16)


def _attn_kernel(q_ref, k_ref, v_ref, o_ref, acc_ref, carry_ref, *, tq, tk, heads):
    qi = pl.program_id(2)

    r = lax.broadcasted_iota(jnp.int32, (2 * tk, 2 * tk), 0)
    c = lax.broadcasted_iota(jnp.int32, (2 * tk, 2 * tk), 1)
    r = jnp.where(r >= tk, r - tk, r)
    sum_mat = jnp.where(c >= tk, 1.0, jnp.where(r > c, 1.0, 0.0)).astype(BF16)

    acc_ref[...] = jnp.zeros_like(acc_ref)
    carry_ref[...] = jnp.zeros_like(carry_ref)

    def block_sums(nl):
        hi = nl.astype(BF16)
        lo = (nl - hi.astype(F32)).astype(BF16)
        return jnp.dot(jnp.concatenate([hi, lo], axis=1), sum_mat, preferred_element_type=F32)

    def pair(kb, masked):
        ks = pl.multiple_of((kb - 1) * tk, tk)
        if masked:
            t_pos = qi * tq + lax.broadcasted_iota(jnp.int32, (tq, 2 * tk), 0)
            s_pos = ks + lax.broadcasted_iota(jnp.int32, (tq, 2 * tk), 1)
            causal = s_pos < t_pos
        for h in range(heads):
            dcol = slice(h * HEAD_DIM, (h + 1) * HEAD_DIM)
            ccol = slice(h * tk, (h + 1) * tk)
            k = k_ref[pl.ds(ks, 2 * tk), dcol]
            v = v_ref[pl.ds(ks, 2 * tk), dcol]
            z = lax.dot_general(q_ref[:, dcol], k, (((1,), (1,)), ((), ())), preferred_element_type=F32)
            nl = jnp.maximum(z, 0.0) + jnp.log2(1.0 + jnp.exp2(-jnp.abs(z)))
            w = z - nl
            if masked:
                nl = jnp.where(causal, nl, 0.0)
            sums_hi = block_sums(nl[:, tk:])
            sums_lo = block_sums(nl[:, :tk])
            carry = carry_ref[:, ccol]
            carry_mid = carry + sums_hi[:, tk:]
            a = jnp.exp2(w - jnp.concatenate([sums_lo[:, :tk] + carry_mid, sums_hi[:, :tk] + carry], axis=1))
            if masked:
                a = jnp.where(causal, a, 0.0)
            acc_ref[:, dcol] += jnp.dot(a.astype(BF16), v, preferred_element_type=F32)
            carry_ref[:, ccol] = carry_mid + sums_lo[:, tk:]

    assert tq == 2 * tk
    diag_kb = 2 * qi + 1

    @pl.when(qi == 0)
    def _():
        pair(diag_kb, masked=True)

    @pl.when(qi > 0)
    def _():
        pair(diag_kb, masked=True)
        pair(diag_kb - 2, masked=False)

    def still_live():
        return (jnp.min(carry_ref[...]) <= F32_POW2_UNDERFLOW).astype(jnp.int32)

    def cond(state):
        kb, live = state
        return jnp.logical_and(kb >= 1, live > 0)

    def body(state):
        kb, _ = state
        pair(kb, masked=False)
        return kb - 2, still_live()

    lax.while_loop(cond, body, (diag_kb - 4, still_live()))
    o_ref[...] = acc_ref[...].astype(o_ref.dtype)


def _attention(proj, batch, seq, *, tq=256, tk=128, heads=8):
    t = proj.shape[0]
    nq = seq // tq
    hg = N_ATTN_HEADS // heads
    width = heads * HEAD_DIM
    kern = functools.partial(_attn_kernel, tq=tq, tk=tk, heads=heads)
    return pl.pallas_call(
        kern,
        out_shape=jax.ShapeDtypeStruct((t, ATTN_WIDTH), BF16),
        grid=(batch, hg, nq),
        in_specs=[
            pl.BlockSpec((tq, width), lambda b, g, i: (b * nq + i, g)),
            pl.BlockSpec((seq, width), lambda b, g, i: (b, hg + g)),
            pl.BlockSpec((seq, width), lambda b, g, i: (b, 2 * hg + g)),
        ],
        out_specs=pl.BlockSpec((tq, width), lambda b, g, i: (b * nq + i, g)),
        scratch_shapes=[pltpu.VMEM((tq, width), F32), pltpu.VMEM((tq, heads * tk), F32)],
        compiler_params=_params("parallel", "parallel", "arbitrary"),
        name="attn",
    )(proj, proj, proj)


def _mix_kernel(attn_ref, u_ref, uh_ref, ga_ref, gp_ref, x_ref, wab_ref, wpg_ref, ps_ref, wpb_ref,
                wout_ref, gpost_ref, o_ref, *, tm, seq):
    i = pl.program_id(0)
    seq_pos0 = (i * tm) % seq
    halo = jnp.where(seq_pos0 == 0, 0.0, uh_ref[...].astype(F32))
    ucat = jnp.concatenate([halo, u_ref[...].astype(F32)], axis=0)

    pos = seq_pos0 + lax.broadcasted_iota(jnp.int32, (tm, 1), 0)
    parts = []
    y_attn_parts = []
    n_groups = len(POOL_WINDOWS)
    d = o_ref.shape[1]
    for g, w in enumerate(POOL_WINDOWS):
        acols = slice(g * d // n_groups, (g + 1) * d // n_groups)
        y_attn_parts.append(jnp.dot(attn_ref[...], wab_ref[:, acols], preferred_element_type=F32))
        cols = slice(g * POOL_GROUP_WIDTH, (g + 1) * POOL_GROUP_WIDTH)
        s = ucat[:, cols]
        k = 1
        while k < w:
            s = s + pltpu.roll(s, k, axis=0)
            k *= 2
        count = jnp.minimum(pos + 1, w).astype(F32)
        pooled = s[HALO:] / count - ucat[HALO:, cols]
        pg = jnp.dot(pooled.astype(BF16), wpg_ref[g], preferred_element_type=F32)
        parts.append((pg * ps_ref[:, cols]).astype(BF16))
    y_attn = jnp.concatenate(y_attn_parts, axis=1)
    y_pool = jnp.dot(jnp.concatenate(parts, axis=1), wpb_ref[...], preferred_element_type=F32)
    mixed = (jax.nn.sigmoid(ga_ref[...].astype(F32)) * y_attn
             + jax.nn.sigmoid(gp_ref[...].astype(F32)) * y_pool)
    y = jnp.dot(mixed.astype(BF16), wout_ref[...], preferred_element_type=F32)
    o_ref[...] = x_ref[...] + _rmsnorm(y, gpost_ref[...])


def _mix(attn, proj, x2d, wab, wpg, pool_scale, wpb, wout, gpost, seq, *, tm=512):
    t, d = x2d.shape
    assert all(w & (w - 1) == 0 and w <= HALO for w in POOL_WINDOWS)
    u_col = 3 * ATTN_WIDTH // POOL_WIDTH
    g_col = (3 * ATTN_WIDTH + POOL_WIDTH) // d
    const = lambda *shape: pl.BlockSpec(shape, lambda i: (0,) * len(shape), pipeline_mode=pl.Buffered(1))
    kern = functools.partial(_mix_kernel, tm=tm, seq=seq)
    return pl.pallas_call(
        kern,
        out_shape=jax.ShapeDtypeStruct((t, d), F32),
        grid=(t // tm,),
        in_specs=[
            pl.BlockSpec((tm, ATTN_WIDTH), lambda i: (i, 0)),
            pl.BlockSpec((tm, POOL_WIDTH), lambda i: (i, u_col)),
            pl.BlockSpec((HALO, POOL_WIDTH), lambda i: (jnp.maximum(i * (tm // HALO) - 1, 0), u_col)),
            pl.BlockSpec((tm, d), lambda i: (i, g_col)),
            pl.BlockSpec((tm, d), lambda i: (i, g_col + 1)),
            pl.BlockSpec((tm, d), lambda i: (i, 0)),
            const(ATTN_WIDTH, d),
            const(len(POOL_WINDOWS), POOL_GROUP_WIDTH, POOL_GROUP_WIDTH),
            const(1, POOL_WIDTH),
            const(POOL_WIDTH, d),
            const(d, d),
            const(1, d),
        ],
        out_specs=pl.BlockSpec((tm, d), lambda i: (i, 0)),
        compiler_params=_params("parallel"),
        name="mix",
    )(attn, proj, proj, proj, proj, x2d, wab, wpg, pool_scale, wpb, wout, gpost)


def _gelu_tanh(x):
    return 0.5 * x * (1.0 + jnp.tanh(0.7978845608028654 * (x + 0.044715 * (x * x * x))))


def _ffn_kernel(x_ref, xh_ref, gpre_ref, wg_ref, wv_ref, cwg_ref, cwv_ref, cbg_ref, cbv_ref, wd_ref,
                gpost_ref, o_ref, h_ref, up_ref, *, tm, seq, n_down):
    i = pl.program_id(0)
    j = pl.program_id(1)

    @pl.when(j == 0)
    def _():
        hh = _rmsnorm(xh_ref[...], gpre_ref[...])
        h_ref[0:HALO, :] = jnp.where((i * tm) % seq == 0, 0.0, hh).astype(BF16)
        h_ref[HALO:, :] = _rmsnorm(x_ref[...], gpre_ref[...]).astype(BF16)
        o_ref[...] = jnp.zeros_like(o_ref)

    h = h_ref[...]

    def conv(slot, w_ref, cw_ref, cb_ref):
        up_ref[slot] = jnp.dot(h, w_ref[...], preferred_element_type=F32)
        out = cb_ref[...]
        for tap in range(CONV_WIDTH):
            off = HALO - (CONV_WIDTH - 1 - tap)
            out = out + up_ref[slot, off:off + tm, :] * cw_ref[tap:tap + 1, :]
        return out

    gate = conv(0, wg_ref, cwg_ref, cbg_ref)
    val = conv(1, wv_ref, cwv_ref, cbv_ref)
    act = (_gelu_tanh(gate) * val).astype(BF16)
    tc = wd_ref.shape[0] // n_down
    for c in range(n_down):
        rows = slice(c * tc, (c + 1) * tc)
        o_ref[...] += jnp.dot(act[:, rows], wd_ref[rows, :], preferred_element_type=F32)

    @pl.when(j == pl.num_programs(1) - 1)
    def _():
        o_ref[...] = x_ref[...] + _rmsnorm(o_ref[...], gpost_ref[...])


def _ffn(x2d, gpre, w_up, conv_w, conv_b, w_down, gpost, seq, *, tm=512, tf=512, n_down=2):
    t, d = x2d.shape
    d_ff = w_down.shape[0]
    nf = d_ff // tf
    kern = functools.partial(_ffn_kernel, tm=tm, seq=seq, n_down=n_down)
    return pl.pallas_call(
        kern,
        out_shape=jax.ShapeDtypeStruct((t, d), F32),
        grid=(t // tm, nf),
        in_specs=[
            pl.BlockSpec((tm, d), lambda i, j: (i, 0)),
            pl.BlockSpec((HALO, d), lambda i, j: (jnp.maximum(i * (tm // HALO) - 1, 0), 0)),
            pl.BlockSpec((1, d), lambda i, j: (0, 0)),
            pl.BlockSpec((d, tf), lambda i, j: (0, j)),
            pl.BlockSpec((d, tf), lambda i, j: (0, nf + j)),
            pl.BlockSpec((CONV_WIDTH, tf), lambda i, j: (0, j)),
            pl.BlockSpec((CONV_WIDTH, tf), lambda i, j: (0, nf + j)),
            pl.BlockSpec((1, tf), lambda i, j: (0, j)),
            pl.BlockSpec((1, tf), lambda i, j: (0, nf + j)),
            pl.BlockSpec((tf, d), lambda i, j: (j, 0)),
            pl.BlockSpec((1, d), lambda i, j: (0, 0)),
        ],
        out_specs=pl.BlockSpec((tm, d), lambda i, j: (i, 0)),
        scratch_shapes=[pltpu.VMEM((HALO + tm, d), BF16),
                        pltpu.VMEM((2, HALO + tm, tf), F32)],
        compiler_params=_params("parallel", "arbitrary"),
        name="ffn",
    )(x2d, x2d, gpre, w_up, w_up, conv_w, conv_w, conv_b, conv_b, w_down, gpost)


def _ple_kernel(x_ref, p_ref, wple_ref, wgate_ref, gpost_ref, o_ref):
    x = x_ref[...]
    e = jnp.dot(p_ref[...].astype(BF16), wple_ref[...], preferred_element_type=F32)
    gate = jnp.dot(x.astype(BF16), wgate_ref[...], preferred_element_type=F32)
    o_ref[...] = x + _rmsnorm(jax.nn.sigmoid(gate) * e, gpost_ref[...])


def _ple(x2d, p2d, w_ple, w_gate, gpost, *, tm=512):
    t, d = x2d.shape
    pd = p2d.shape[1]
    return pl.pallas_call(
        _ple_kernel,
        out_shape=jax.ShapeDtypeStruct((t, d), F32),
        grid=(t // tm,),
        in_specs=[
            pl.BlockSpec((tm, d), lambda i: (i, 0)),
            pl.BlockSpec((tm, pd), lambda i: (i, 0)),
            pl.BlockSpec((pd, d), lambda i: (0, 0)),
            pl.BlockSpec((d, d), lambda i: (0, 0)),
            pl.BlockSpec((1, d), lambda i: (0, 0)),
        ],
        out_specs=pl.BlockSpec((tm, d), lambda i: (i, 0)),
        compiler_params=_params("parallel"),
        name="ple",
    )(x2d, p2d, w_ple, w_gate, gpost)


def kernel(x, p, norm_mix_pre, w_in, w_attn_branch, w_pool_group, pool_scale, w_pool_branch, w_out,
           norm_mix_post, norm_ffn_pre, w_up, conv_w, conv_b, w_down, norm_ffn_post, w_ple, w_ple_gate,
           norm_ple_post):
    batch, seq, d = x.shape
    depth = w_in.shape[0]
    x2d = x.reshape(batch * seq, d)
    row = lambda a: a.reshape(1, -1)
    for l in range(depth):
        proj = _in_proj(x2d, row(norm_mix_pre[l]), w_in[l].astype(BF16))
        attn = _attention(proj, batch, seq)
        x2d = _mix(attn, proj, x2d, w_attn_branch[l].astype(BF16), w_pool_group[l].astype(BF16),
                   row(pool_scale[l]), w_pool_branch[l].astype(BF16), w_out[l].astype(BF16),
                   row(norm_mix_post[l]), seq)
        x2d = _ffn(x2d, row(norm_ffn_pre[l]), w_up[l].astype(BF16), conv_w[l], row(conv_b[l]),
                   w_down[l].astype(BF16), row(norm_ffn_post[l]), seq)
        x2d = _ple(x2d, p[l].reshape(batch * seq, -1), w_ple[l].astype(BF16),
                   w_ple_gate[l].astype(BF16), row(norm_ple_post[l]))
    return x2d.reshape(batch, seq, d)
```

```python
import functools
import math

import jax
import jax.numpy as jnp
from jax import lax
from jax.experimental import pallas as pl
from jax.experimental.pallas import tpu as pltpu

N_ATTN_HEADS = 8
HEAD_DIM = 128
ATTN_WIDTH = N_ATTN_HEADS * HEAD_DIM
POOL_WINDOWS = (2, 4, 8, 16)
POOL_GROUP_WIDTH = 256
POOL_WIDTH = len(POOL_WINDOWS) * POOL_GROUP_WIDTH
CONV_WIDTH = 3
EPS = 1e-6

HALO = 16

Q_SCALE = HEAD_DIM ** -0.5 * math.log2(math.e)
F32_POW2_UNDERFLOW = 150.0

VMEM_LIMIT_BYTES = 56 * 1024 * 1024

LANES = 128

F32 = jnp.float32
BF16 = jnp.bfloat16


def _rmsnorm(x, gain):
    return x * lax.rsqrt(jnp.mean(x * x, axis=-1, keepdims=True) + EPS) * gain


def _rms_scale(src_ref, rinv_ref):
    @pl.when(pl.program_id(0) >= 0)
    def _():
        y = src_ref[...]
        ms = jnp.mean(y * y, axis=-1, keepdims=True)
        rinv_ref[...] = jnp.broadcast_to(lax.rsqrt(ms + EPS), rinv_ref.shape)

    return jnp.concatenate([rinv_ref[...]] * (src_ref.shape[1] // rinv_ref.shape[1]), axis=1)


def _params(*semantics):
    return pltpu.CompilerParams(dimension_semantics=semantics, vmem_limit_bytes=VMEM_LIMIT_BYTES)


def _in_proj_kernel(x_ref, g_ref, w_ref, o_ref, h_ref, *, tn):
    j = pl.program_id(1)

    @pl.when(j == 0)
    def _():
        h_ref[...] = _rmsnorm(x_ref[...], g_ref[...]).astype(BF16)

    col = j * tn + lax.broadcasted_iota(jnp.int32, (1, tn), 1)
    col_scale = jnp.where(col < ATTN_WIDTH, Q_SCALE, 1.0)
    y = jnp.dot(h_ref[...], w_ref[...], preferred_element_type=F32)
    o_ref[...] = (y * col_scale).astype(o_ref.dtype)


def _in_proj(x2d, gain, w_bf16, *, tm=1024, tn=2048):
    t, d = x2d.shape
    n = w_bf16.shape[1]
    return pl.pallas_call(
        functools.partial(_in_proj_kernel, tn=tn),
        out_shape=jax.ShapeDtypeStruct((t, n), BF16),
        grid=(t // tm, n // tn),
        in_specs=[
            pl.BlockSpec((tm, d), lambda i, j: (i, 0)),
            pl.BlockSpec((1, d), lambda i, j: (0, 0)),
            pl.BlockSpec((d, tn), lambda i, j: (0, j)),
        ],
        out_specs=pl.BlockSpec((tm, tn), lambda i, j: (i, j)),
        scratch_shapes=[pltpu.VMEM((tm, d), BF16)],
        compiler_params=_params("parallel", "arbitrary"),
        name="in_proj",
    )(x2d, gain, w_bf16)


def _attn_kernel(q_ref, k_ref, v_ref, o_ref, acc_ref, carry_ref, *, tq, tk, heads):
    qi = pl.program_id(2)

    r = lax.broadcasted_iota(jnp.int32, (2 * tk, 2 * tk), 0)
    c = lax.broadcasted_iota(jnp.int32, (2 * tk, 2 * tk), 1)
    r = jnp.where(r >= tk, r - tk, r)
    sum_mat = jnp.where(c >= tk, 1.0, jnp.where(r > c, 1.0, 0.0)).astype(BF16)

    acc_ref[...] = jnp.zeros_like(acc_ref)
    carry_ref[...] = jnp.zeros_like(carry_ref)

    def block_sums(nl):
        hi = nl.astype(BF16)
        lo = (nl - hi.astype(F32)).astype(BF16)
        return jnp.dot(jnp.concatenate([hi, lo], axis=1), sum_mat, preferred_element_type=F32)

    def pair(kb, masked):
        ks = pl.multiple_of((kb - 1) * tk, tk)
        if masked:
            t_pos = qi * tq + lax.broadcasted_iota(jnp.int32, (tq, 2 * tk), 0)
            s_pos = ks + lax.broadcasted_iota(jnp.int32, (tq, 2 * tk), 1)
            causal = s_pos < t_pos
        for h in range(heads):
            dcol = slice(h * HEAD_DIM, (h + 1) * HEAD_DIM)
            ccol = slice(h * tk, (h + 1) * tk)
            k = k_ref[pl.ds(ks, 2 * tk), dcol]
            v = v_ref[pl.ds(ks, 2 * tk), dcol]
            z = lax.dot_general(q_ref[:, dcol], k, (((1,), (1,)), ((), ())), preferred_element_type=F32)
            nl = jnp.maximum(z, 0.0) + jnp.log2(1.0 + jnp.exp2(-jnp.abs(z)))
            w = z - nl
            if masked:
                nl = jnp.where(causal, nl, 0.0)
            sums_hi = block_sums(nl[:, tk:])
            sums_lo = block_sums(nl[:, :tk])
            carry = carry_ref[:, ccol]
            carry_mid = carry + sums_hi[:, tk:]
            a = jnp.exp2(w - jnp.concatenate([sums_lo[:, :tk] + carry_mid, sums_hi[:, :tk] + carry], axis=1))
            if masked:
                a = jnp.where(causal, a, 0.0)
            acc_ref[:, dcol] += jnp.dot(a.astype(BF16), v, preferred_element_type=F32)
            carry_ref[:, ccol] = carry_mid + sums_lo[:, tk:]

    assert tq == 2 * tk
    diag_kb = 2 * qi + 1

    @pl.when(qi == 0)
    def _():
        pair(diag_kb, masked=True)

    @pl.when(qi > 0)
    def _():
        pair(diag_kb, masked=True)
        pair(diag_kb - 2, masked=False)

    def still_live():
        return (jnp.min(carry_ref[...]) <= F32_POW2_UNDERFLOW).astype(jnp.int32)

    def cond(state):
        kb, live = state
        return jnp.logical_and(kb >= 1, live > 0)

    def body(state):
        kb, _ = state
        pair(kb, masked=False)
        return kb - 2, still_live()

    lax.while_loop(cond, body, (diag_kb - 4, still_live()))
    o_ref[...] = acc_ref[...].astype(o_ref.dtype)


def _attention(proj, batch, seq, *, tq=256, tk=128, heads=8):
    t = proj.shape[0]
    nq = seq // tq
    hg = N_ATTN_HEADS // heads
    width = heads * HEAD_DIM
    kern = functools.partial(_attn_kernel, tq=tq, tk=tk, heads=heads)
    return pl.pallas_call(
        kern,
        out_shape=jax.ShapeDtypeStruct((t, ATTN_WIDTH), BF16),
        grid=(batch, hg, nq),
        in_specs=[
            pl.BlockSpec((tq, width), lambda b, g, i: (b * nq + i, g)),
            pl.BlockSpec((seq, width), lambda b, g, i: (b, hg + g)),
            pl.BlockSpec((seq, width), lambda b, g, i: (b, 2 * hg + g)),
        ],
        out_specs=pl.BlockSpec((tq, width), lambda b, g, i: (b * nq + i, g)),
        scratch_shapes=[pltpu.VMEM((tq, width), F32), pltpu.VMEM((tq, heads * tk), F32)],
        compiler_params=_params("parallel", "parallel", "arbitrary"),
        name="attn",
    )(proj, proj, proj)


def _mix_kernel(attn_ref, u_ref, uh_ref, ga_ref, gp_ref, x_ref, wab_ref, wpg_ref, ps_ref, wpb_ref,
                wout_ref, gpost_ref, o_ref, *, tm, seq):
    i = pl.program_id(0)
    seq_pos0 = (i * tm) % seq
    halo = jnp.where(seq_pos0 == 0, 0.0, uh_ref[...].astype(F32))
    ucat = jnp.concatenate([halo, u_ref[...].astype(F32)], axis=0)

    pos = seq_pos0 + lax.broadcasted_iota(jnp.int32, (tm, 1), 0)
    parts = []
    y_attn_parts = []
    n_groups = len(POOL_WINDOWS)
    d = o_ref.shape[1]
    for g, w in enumerate(POOL_WINDOWS):
        acols = slice(g * d // n_groups, (g + 1) * d // n_groups)
        y_attn_parts.append(jnp.dot(attn_ref[...], wab_ref[:, acols], preferred_element_type=F32))
        cols = slice(g * POOL_GROUP_WIDTH, (g + 1) * POOL_GROUP_WIDTH)
        s = ucat[:, cols]
        k = 1
        while k < w:
            s = s + pltpu.roll(s, k, axis=0)
            k *= 2
        count = jnp.minimum(pos + 1, w).astype(F32)
        pooled = s[HALO:] / count - ucat[HALO:, cols]
        pg = jnp.dot(pooled.astype(BF16), wpg_ref[g], preferred_element_type=F32)
        parts.append((pg * ps_ref[:, cols]).astype(BF16))
    y_attn = jnp.concatenate(y_attn_parts, axis=1)
    y_pool = jnp.dot(jnp.concatenate(parts, axis=1), wpb_ref[...], preferred_element_type=F32)
    mixed = (jax.nn.sigmoid(ga_ref[...].astype(F32)) * y_attn
             + jax.nn.sigmoid(gp_ref[...].astype(F32)) * y_pool)
    y = jnp.dot(mixed.astype(BF16), wout_ref[...], preferred_element_type=F32)
    o_ref[...] = x_ref[...] + _rmsnorm(y, gpost_ref[...])


def _mix(attn, proj, x2d, wab, wpg, pool_scale, wpb, wout, gpost, seq, *, tm=512):
    t, d = x2d.shape
    assert all(w & (w - 1) == 0 and w <= HALO for w in POOL_WINDOWS)
    u_col = 3 * ATTN_WIDTH // POOL_WIDTH
    g_col = (3 * ATTN_WIDTH + POOL_WIDTH) // d
    const = lambda *shape: pl.BlockSpec(shape, lambda i: (0,) * len(shape), pipeline_mode=pl.Buffered(1))
    kern = functools.partial(_mix_kernel, tm=tm, seq=seq)
    return pl.pallas_call(
        kern,
        out_shape=jax.ShapeDtypeStruct((t, d), F32),
        grid=(t // tm,),
        in_specs=[
            pl.BlockSpec((tm, ATTN_WIDTH), lambda i: (i, 0)),
            pl.BlockSpec((tm, POOL_WIDTH), lambda i: (i, u_col)),
            pl.BlockSpec((HALO, POOL_WIDTH), lambda i: (jnp.maximum(i * (tm // HALO) - 1, 0), u_col)),
            pl.BlockSpec((tm, d), lambda i: (i, g_col)),
            pl.BlockSpec((tm, d), lambda i: (i, g_col + 1)),
            pl.BlockSpec((tm, d), lambda i: (i, 0)),
            const(ATTN_WIDTH, d),
            const(len(POOL_WINDOWS), POOL_GROUP_WIDTH, POOL_GROUP_WIDTH),
            const(1, POOL_WIDTH),
            const(POOL_WIDTH, d),
            const(d, d),
            const(1, d),
        ],
        out_specs=pl.BlockSpec((tm, d), lambda i: (i, 0)),
        compiler_params=_params("parallel"),
        name="mix",
    )(attn, proj, proj, proj, proj, x2d, wab, wpg, pool_scale, wpb, wout, gpost)


def _gelu_tanh(x):
    return 0.5 * x * (1.0 + jnp.tanh(0.7978845608028654 * (x + 0.044715 * (x * x * x))))


def _ffn_kernel(x_ref, xh_ref, gpre_ref, wg_ref, wv_ref, conv_ref, wd_ref, gpost_ref, o_ref, h_ref, up_ref,
                rinv_ref, *, tm, seq, n_down):
    i = pl.program_id(0)
    j = pl.program_id(1)
    nf = pl.num_programs(1)

    @pl.when(j == 0)
    def _():
        hh = _rmsnorm(xh_ref[...], gpre_ref[...])
        h_ref[0:HALO, :] = jnp.where((i * tm) % seq == 0, 0.0, hh).astype(BF16)
        h_ref[HALO:, :] = _rmsnorm(x_ref[...], gpre_ref[...]).astype(BF16)
        o_ref[...] = jnp.zeros_like(o_ref)

    h = h_ref[...]

    def conv(slot, w_ref, chunk):
        up_ref[slot] = jnp.dot(h, w_ref[...], preferred_element_type=F32)
        taps = conv_ref[chunk]
        out = taps[CONV_WIDTH:CONV_WIDTH + 1, :]
        for tap in range(CONV_WIDTH):
            off = HALO - (CONV_WIDTH - 1 - tap)
            out = out + up_ref[slot, off:off + tm, :] * taps[tap:tap + 1, :]
        return out

    gate = conv(0, wg_ref, j)
    val = conv(1, wv_ref, nf + j)
    act = (_gelu_tanh(gate) * val).astype(BF16)
    tc = wd_ref.shape[0] // n_down
    for c in range(n_down):
        rows = slice(c * tc, (c + 1) * tc)
        o_ref[...] += jnp.dot(act[:, rows], wd_ref[rows, :], preferred_element_type=F32)

    @pl.when(j == pl.num_programs(1) - 1)
    def _():
        scale = _rms_scale(o_ref, rinv_ref)
        o_ref[...] = x_ref[...] + o_ref[...] * scale * gpost_ref[...]


def _ffn(x2d, gpre, w_up, conv_w, conv_b, w_down, gpost, seq, *, tm=512, tf=512, n_down=2):
    t, d = x2d.shape
    d_ff = w_down.shape[0]
    nf = d_ff // tf
    kern = functools.partial(_ffn_kernel, tm=tm, seq=seq, n_down=n_down)
    conv_taps = jnp.concatenate([conv_w, conv_b], axis=0).reshape(CONV_WIDTH + 1, 2 * nf, tf)
    conv_taps = conv_taps.transpose(1, 0, 2)
    return pl.pallas_call(
        kern,
        out_shape=jax.ShapeDtypeStruct((t, d), F32),
        grid=(t // tm, nf),
        in_specs=[
            pl.BlockSpec((tm, d), lambda i, j: (i, 0)),
            pl.BlockSpec((HALO, d), lambda i, j: (jnp.maximum(i * (tm // HALO) - 1, 0), 0)),
            pl.BlockSpec((1, d), lambda i, j: (0, 0)),
            pl.BlockSpec((d, tf), lambda i, j: (0, j)),
            pl.BlockSpec((d, tf), lambda i, j: (0, nf + j)),
            pl.BlockSpec((2 * nf, CONV_WIDTH + 1, tf), lambda i, j: (0, 0, 0)),
            pl.BlockSpec((tf, d), lambda i, j: (j, 0)),
            pl.BlockSpec((1, d), lambda i, j: (0, 0)),
        ],
        out_specs=pl.BlockSpec((tm, d), lambda i, j: (i, 0)),
        scratch_shapes=[pltpu.VMEM((HALO + tm, d), BF16),
                        pltpu.VMEM((2, HALO + tm, tf), F32),
                        pltpu.VMEM((tm, LANES), F32)],
        compiler_params=_params("parallel", "arbitrary"),
        name="ffn",
    )(x2d, x2d, gpre, w_up, w_up, conv_taps, w_down, gpost)


def _ple_kernel(x_ref, p_ref, wple_ref, wgate_ref, gpost_ref, o_ref):
    x = x_ref[...]
    e = jnp.dot(p_ref[...].astype(BF16), wple_ref[...], preferred_element_type=F32)
    gate = jnp.dot(x.astype(BF16), wgate_ref[...], preferred_element_type=F32)
    o_ref[...] = x + _rmsnorm(jax.nn.sigmoid(gate) * e, gpost_ref[...])


def _ple(x2d, p2d, w_ple, w_gate, gpost, *, tm=512):
    t, d = x2d.shape
    pd = p2d.shape[1]
    return pl.pallas_call(
        _ple_kernel,
        out_shape=jax.ShapeDtypeStruct((t, d), F32),
        grid=(t // tm,),
        in_specs=[
            pl.BlockSpec((tm, d), lambda i: (i, 0)),
            pl.BlockSpec((tm, pd), lambda i: (i, 0)),
            pl.BlockSpec((pd, d), lambda i: (0, 0)),
            pl.BlockSpec((d, d), lambda i: (0, 0)),
            pl.BlockSpec((1, d), lambda i: (0, 0)),
        ],
        out_specs=pl.BlockSpec((tm, d), lambda i: (i, 0)),
        compiler_params=_params("parallel"),
        name="ple",
    )(x2d, p2d, w_ple, w_gate, gpost)


def kernel(x, p, norm_mix_pre, w_in, w_attn_branch, w_pool_group, pool_scale, w_pool_branch, w_out,
           norm_mix_post, norm_ffn_pre, w_up, conv_w, conv_b, w_down, norm_ffn_post, w_ple, w_ple_gate,
           norm_ple_post):
    batch, seq, d = x.shape
    depth = w_in.shape[0]
    x2d = x.reshape(batch * seq, d)
    row = lambda a: a.reshape(1, -1)
    for l in range(depth):
        proj = _in_proj(x2d, row(norm_mix_pre[l]), w_in[l].astype(BF16))
        attn = _attention(proj, batch, seq)
        x2d = _mix(attn, proj, x2d, w_attn_branch[l].astype(BF16), w_pool_group[l].astype(BF16),
                   row(pool_scale[l]), w_pool_branch[l].astype(BF16), w_out[l].astype(BF16),
                   row(norm_mix_post[l]), seq)
        x2d = _ffn(x2d, row(norm_ffn_pre[l]), w_up[l].astype(BF16), conv_w[l], row(conv_b[l]),
                   w_down[l].astype(BF16), row(norm_ffn_post[l]), seq)
        x2d = _ple(x2d, p[l].reshape(batch * seq, -1), w_ple[l].astype(BF16),
                   w_ple_gate[l].astype(BF16), row(norm_ple_post[l]))
    return x2d.reshape(batch, seq, d)
```

```python
import functools
import math

import jax
import jax.numpy as jnp
from jax import lax
from jax.experimental import pallas as pl
from jax.experimental.pallas import tpu as pltpu

N_ATTN_HEADS = 8
HEAD_DIM = 128
ATTN_WIDTH = N_ATTN_HEADS * HEAD_DIM
POOL_WINDOWS = (2, 4, 8, 16)
POOL_GROUP_WIDTH = 256
POOL_WIDTH = len(POOL_WINDOWS) * POOL_GROUP_WIDTH
CONV_WIDTH = 3
EPS = 1e-6

HALO = 16

Q_SCALE = HEAD_DIM ** -0.5 * math.log2(math.e)
F32_POW2_UNDERFLOW = 150.0

VMEM_LIMIT_BYTES = 56 * 1024 * 1024

LANES = 128
BF16_SUBLANES = 16

F32 = jnp.float32
BF16 = jnp.bfloat16


def _rmsnorm(x, gain):
    return x * lax.rsqrt(jnp.mean(x * x, axis=-1, keepdims=True) + EPS) * gain


def _rms_scale(src_ref, rinv_ref):
    @pl.when(pl.program_id(0) >= 0)
    def _():
        y = src_ref[...]
        ms = jnp.mean(y * y, axis=-1, keepdims=True)
        rinv_ref[...] = jnp.broadcast_to(lax.rsqrt(ms + EPS), rinv_ref.shape)

    return jnp.concatenate([rinv_ref[...]] * (src_ref.shape[1] // rinv_ref.shape[1]), axis=1)


def _params(*semantics):
    return pltpu.CompilerParams(dimension_semantics=semantics, vmem_limit_bytes=VMEM_LIMIT_BYTES)


def _side_cast_specs(weights, grid, step_of):
    n_steps = math.prod(grid)
    in_specs, out_specs, out_shapes = [], [], []
    for w in weights:
        rows, cols = w.shape
        n_blocks = math.gcd(n_steps, rows // BF16_SUBLANES)
        stride = n_steps // n_blocks
        index = lambda *g, stride=stride: (step_of(*g) // stride, 0)
        in_specs.append(pl.BlockSpec((rows // n_blocks, cols), index))
        out_specs.append(pl.BlockSpec((rows // n_blocks, cols), index))
        out_shapes.append(jax.ShapeDtypeStruct((rows, cols), BF16))
    return in_specs, out_specs, out_shapes


def _cast_sides(src_refs, dst_refs):
    for src, dst in zip(src_refs, dst_refs):
        dst[...] = src[...].astype(BF16)


def _in_proj_kernel(x_ref, g_ref, w_ref, *refs, tn, n_side):
    side_src, (o_ref, *side_dst), h_ref = refs[:n_side], refs[n_side:2 * n_side + 1], refs[-1]
    j = pl.program_id(1)
    _cast_sides(side_src, side_dst)

    @pl.when(j == 0)
    def _():
        h_ref[...] = _rmsnorm(x_ref[...], g_ref[...]).astype(BF16)

    col = j * tn + lax.broadcasted_iota(jnp.int32, (1, tn), 1)
    col_scale = jnp.where(col < ATTN_WIDTH, Q_SCALE, 1.0)
    y = jnp.dot(h_ref[...], w_ref[...], preferred_element_type=F32)
    o_ref[...] = (y * col_scale).astype(o_ref.dtype)


def _in_proj(x2d, gain, w_bf16, side, *, tm=1024, tn=2048):
    t, d = x2d.shape
    n = w_bf16.shape[1]
    grid = (t // tm, n // tn)
    side_in, side_out, side_shapes = _side_cast_specs(side, grid, lambda i, j: i * grid[1] + j)
    proj, *side_bf16 = pl.pallas_call(
        functools.partial(_in_proj_kernel, tn=tn, n_side=len(side)),
        out_shape=[jax.ShapeDtypeStruct((t, n), BF16)] + side_shapes,
        grid=grid,
        in_specs=[
            pl.BlockSpec((tm, d), lambda i, j: (i, 0)),
            pl.BlockSpec((1, d), lambda i, j: (0, 0)),
            pl.BlockSpec((d, tn), lambda i, j: (0, j)),
        ] + side_in,
        out_specs=[pl.BlockSpec((tm, tn), lambda i, j: (i, j))] + side_out,
        scratch_shapes=[pltpu.VMEM((tm, d), BF16)],
        compiler_params=_params("arbitrary", "arbitrary"),
        name="in_proj",
    )(x2d, gain, w_bf16, *side)
    return proj, side_bf16


def _attn_kernel(q_ref, k_ref, v_ref, *refs, tq, tk, heads, n_side):
    side_src, (o_ref, *side_dst), (acc_ref, carry_ref) = refs[:n_side], refs[n_side:2 * n_side + 1], refs[-2:]
    qi = pl.program_id(2)
    _cast_sides(side_src, side_dst)

    r = lax.broadcasted_iota(jnp.int32, (2 * tk, 2 * tk), 0)
    c = lax.broadcasted_iota(jnp.int32, (2 * tk, 2 * tk), 1)
    r = jnp.where(r >= tk, r - tk, r)
    sum_mat = jnp.where(c >= tk, 1.0, jnp.where(r > c, 1.0, 0.0)).astype(BF16)

    acc_ref[...] = jnp.zeros_like(acc_ref)
    carry_ref[...] = jnp.zeros_like(carry_ref)

    def block_sums(nl):
        hi = nl.astype(BF16)
        lo = (nl - hi.astype(F32)).astype(BF16)
        return jnp.dot(jnp.concatenate([hi, lo], axis=1), sum_mat, preferred_element_type=F32)

    def pair(kb, masked):
        ks = pl.multiple_of((kb - 1) * tk, tk)
        if masked:
            t_pos = qi * tq + lax.broadcasted_iota(jnp.int32, (tq, 2 * tk), 0)
            s_pos = ks + lax.broadcasted_iota(jnp.int32, (tq, 2 * tk), 1)
            causal = s_pos < t_pos
        for h in range(heads):
            dcol = slice(h * HEAD_DIM, (h + 1) * HEAD_DIM)
            ccol = slice(h * tk, (h + 1) * tk)
            k = k_ref[pl.ds(ks, 2 * tk), dcol]
            v = v_ref[pl.ds(ks, 2 * tk), dcol]
            z = lax.dot_general(q_ref[:, dcol], k, (((1,), (1,)), ((), ())), preferred_element_type=F32)
            nl = jnp.maximum(z, 0.0) + jnp.log2(1.0 + jnp.exp2(-jnp.abs(z)))
            w = z - nl
            if masked:
                nl = jnp.where(causal, nl, 0.0)
            sums_hi = block_sums(nl[:, tk:])
            sums_lo = block_sums(nl[:, :tk])
            carry = carry_ref[:, ccol]
            carry_mid = carry + sums_hi[:, tk:]
            a = jnp.exp2(w - jnp.concatenate([sums_lo[:, :tk] + carry_mid, sums_hi[:, :tk] + carry], axis=1))
            if masked:
                a = jnp.where(causal, a, 0.0)
            acc_ref[:, dcol] += jnp.dot(a.astype(BF16), v, preferred_element_type=F32)
            carry_ref[:, ccol] = carry_mid + sums_lo[:, tk:]

    assert tq == 2 * tk
    diag_kb = 2 * qi + 1

    @pl.when(qi == 0)
    def _():
        pair(diag_kb, masked=True)

    @pl.when(qi > 0)
    def _():
        pair(diag_kb, masked=True)
        pair(diag_kb - 2, masked=False)

    def still_live():
        return (jnp.min(carry_ref[...]) <= F32_POW2_UNDERFLOW).astype(jnp.int32)

    def cond(state):
        kb, live = state
        return jnp.logical_and(kb >= 1, live > 0)

    def body(state):
        kb, _ = state
        pair(kb, masked=False)
        return kb - 2, still_live()

    lax.while_loop(cond, body, (diag_kb - 4, still_live()))
    o_ref[...] = acc_ref[...].astype(o_ref.dtype)


def _attention(proj, batch, seq, side, *, tq=256, tk=128, heads=8):
    t = proj.shape[0]
    nq = seq // tq
    hg = N_ATTN_HEADS // heads
    width = heads * HEAD_DIM
    grid = (batch, hg, nq)
    side_in, side_out, side_shapes = _side_cast_specs(side, grid, lambda b, g, i: (b * hg + g) * nq + i)
    kern = functools.partial(_attn_kernel, tq=tq, tk=tk, heads=heads, n_side=len(side))
    attn, *side_bf16 = pl.pallas_call(
        kern,
        out_shape=[jax.ShapeDtypeStruct((t, ATTN_WIDTH), BF16)] + side_shapes,
        grid=grid,
        in_specs=[
            pl.BlockSpec((tq, width), lambda b, g, i: (b * nq + i, g)),
            pl.BlockSpec((seq, width), lambda b, g, i: (b, hg + g)),
            pl.BlockSpec((seq, width), lambda b, g, i: (b, 2 * hg + g)),
        ] + side_in,
        out_specs=[pl.BlockSpec((tq, width), lambda b, g, i: (b * nq + i, g))] + side_out,
        scratch_shapes=[pltpu.VMEM((tq, width), F32), pltpu.VMEM((tq, heads * tk), F32)],
        compiler_params=_params("arbitrary", "arbitrary", "arbitrary"),
        name="attn",
    )(proj, proj, proj, *side)
    return attn, side_bf16


def _mix_kernel(attn_ref, u_ref, uh_ref, ga_ref, gp_ref, x_ref, wab_ref, wpg_ref, ps_ref, wpb_ref,
                wout_ref, gpost_ref, *refs, tm, seq, n_side):
    side_src, (o_ref, *side_dst) = refs[:n_side], refs[n_side:]
    i = pl.program_id(0)
    _cast_sides(side_src, side_dst)
    seq_pos0 = (i * tm) % seq
    halo = jnp.where(seq_pos0 == 0, 0.0, uh_ref[...].astype(F32))
    ucat = jnp.concatenate([halo, u_ref[...].astype(F32)], axis=0)

    pos = seq_pos0 + lax.broadcasted_iota(jnp.int32, (tm, 1), 0)
    parts = []
    y_attn_parts = []
    n_groups = len(POOL_WINDOWS)
    d = o_ref.shape[1]
    for g, w in enumerate(POOL_WINDOWS):
        acols = slice(g * d // n_groups, (g + 1) * d // n_groups)
        y_attn_parts.append(jnp.dot(attn_ref[...], wab_ref[:, acols], preferred_element_type=F32))
        cols = slice(g * POOL_GROUP_WIDTH, (g + 1) * POOL_GROUP_WIDTH)
        s = ucat[:, cols]
        k = 1
        while k < w:
            s = s + pltpu.roll(s, k, axis=0)
            k *= 2
        count = jnp.minimum(pos + 1, w).astype(F32)
        pooled = s[HALO:] / count - ucat[HALO:, cols]
        pg = jnp.dot(pooled.astype(BF16), wpg_ref[g], preferred_element_type=F32)
        parts.append((pg * ps_ref[:, cols]).astype(BF16))
    y_attn = jnp.concatenate(y_attn_parts, axis=1)
    y_pool = jnp.dot(jnp.concatenate(parts, axis=1), wpb_ref[...], preferred_element_type=F32)
    mixed = (jax.nn.sigmoid(ga_ref[...].astype(F32)) * y_attn
             + jax.nn.sigmoid(gp_ref[...].astype(F32)) * y_pool)
    y = jnp.dot(mixed.astype(BF16), wout_ref[...], preferred_element_type=F32)
    o_ref[...] = x_ref[...] + _rmsnorm(y, gpost_ref[...])


def _mix(attn, proj, x2d, wab, wpg, pool_scale, wpb, wout, gpost, seq, side, *, tm=512):
    t, d = x2d.shape
    assert all(w & (w - 1) == 0 and w <= HALO for w in POOL_WINDOWS)
    u_col = 3 * ATTN_WIDTH // POOL_WIDTH
    g_col = (3 * ATTN_WIDTH + POOL_WIDTH) // d
    const = lambda *shape: pl.BlockSpec(shape, lambda i: (0,) * len(shape), pipeline_mode=pl.Buffered(1))
    grid = (t // tm,)
    side_in, side_out, side_shapes = _side_cast_specs(side, grid, lambda i: i)
    kern = functools.partial(_mix_kernel, tm=tm, seq=seq, n_side=len(side))
    x1, *side_bf16 = pl.pallas_call(
        kern,
        out_shape=[jax.ShapeDtypeStruct((t, d), F32)] + side_shapes,
        grid=grid,
        in_specs=[
            pl.BlockSpec((tm, ATTN_WIDTH), lambda i: (i, 0)),
            pl.BlockSpec((tm, POOL_WIDTH), lambda i: (i, u_col)),
            pl.BlockSpec((HALO, POOL_WIDTH), lambda i: (jnp.maximum(i * (tm // HALO) - 1, 0), u_col)),
            pl.BlockSpec((tm, d), lambda i: (i, g_col)),
            pl.BlockSpec((tm, d), lambda i: (i, g_col + 1)),
            pl.BlockSpec((tm, d), lambda i: (i, 0)),
            const(ATTN_WIDTH, d),
            const(len(POOL_WINDOWS), POOL_GROUP_WIDTH, POOL_GROUP_WIDTH),
            const(1, POOL_WIDTH),
            const(POOL_WIDTH, d),
            const(d, d),
            const(1, d),
        ] + side_in,
        out_specs=[pl.BlockSpec((tm, d), lambda i: (i, 0))] + side_out,
        compiler_params=_params("arbitrary"),
        name="mix",
    )(attn, proj, proj, proj, proj, x2d, wab, wpg, pool_scale, wpb, wout, gpost, *side)
    return x1, side_bf16


def _gelu_tanh(x):
    return 0.5 * x * (1.0 + jnp.tanh(0.7978845608028654 * (x + 0.044715 * (x * x * x))))


def _ffn_kernel(x_ref, xh_ref, gpre_ref, wg_ref, wv_ref, conv_ref, wd_ref, gpost_ref, o_ref, h_ref, up_ref,
                rinv_ref, *, tm, seq, n_down):
    i = pl.program_id(0)
    j = pl.program_id(1)
    nf = pl.num_programs(1)

    @pl.when(j == 0)
    def _():
        hh = _rmsnorm(xh_ref[...], gpre_ref[...])
        h_ref[0:HALO, :] = jnp.where((i * tm) % seq == 0, 0.0, hh).astype(BF16)
        h_ref[HALO:, :] = _rmsnorm(x_ref[...], gpre_ref[...]).astype(BF16)
        o_ref[...] = jnp.zeros_like(o_ref)

    h = h_ref[...]

    def conv(slot, w_ref, chunk):
        up_ref[slot] = jnp.dot(h, w_ref[...], preferred_element_type=F32)
        taps = conv_ref[chunk]
        out = taps[CONV_WIDTH:CONV_WIDTH + 1, :]
        for tap in range(CONV_WIDTH):
            off = HALO - (CONV_WIDTH - 1 - tap)
            out = out + up_ref[slot, off:off + tm, :] * taps[tap:tap + 1, :]
        return out

    gate = conv(0, wg_ref, j)
    val = conv(1, wv_ref, nf + j)
    act = (_gelu_tanh(gate) * val).astype(BF16)
    tc = wd_ref.shape[0] // n_down
    for c in range(n_down):
        rows = slice(c * tc, (c + 1) * tc)
        o_ref[...] += jnp.dot(act[:, rows], wd_ref[rows, :], preferred_element_type=F32)

    @pl.when(j == pl.num_programs(1) - 1)
    def _():
        scale = _rms_scale(o_ref, rinv_ref)
        o_ref[...] = x_ref[...] + o_ref[...] * scale * gpost_ref[...]


def _ffn(x2d, gpre, w_up, conv_w, conv_b, w_down, gpost, seq, *, tm=512, tf=512, n_down=2):
    t, d = x2d.shape
    d_ff = w_down.shape[0]
    nf = d_ff // tf
    kern = functools.partial(_ffn_kernel, tm=tm, seq=seq, n_down=n_down)
    conv_taps = jnp.concatenate([conv_w, conv_b], axis=0).reshape(CONV_WIDTH + 1, 2 * nf, tf)
    conv_taps = conv_taps.transpose(1, 0, 2)
    return pl.pallas_call(
        kern,
        out_shape=jax.ShapeDtypeStruct((t, d), F32),
        grid=(t // tm, nf),
        in_specs=[
            pl.BlockSpec((tm, d), lambda i, j: (i, 0)),
            pl.BlockSpec((HALO, d), lambda i, j: (jnp.maximum(i * (tm // HALO) - 1, 0), 0)),
            pl.BlockSpec((1, d), lambda i, j: (0, 0)),
            pl.BlockSpec((d, tf), lambda i, j: (0, j)),
            pl.BlockSpec((d, tf), lambda i, j: (0, nf + j)),
            pl.BlockSpec((2 * nf, CONV_WIDTH + 1, tf), lambda i, j: (0, 0, 0)),
            pl.BlockSpec((tf, d), lambda i, j: (j, 0)),
            pl.BlockSpec((1, d), lambda i, j: (0, 0)),
        ],
        out_specs=pl.BlockSpec((tm, d), lambda i, j: (i, 0)),
        scratch_shapes=[pltpu.VMEM((HALO + tm, d), BF16),
                        pltpu.VMEM((2, HALO + tm, tf), F32),
                        pltpu.VMEM((tm, LANES), F32)],
        compiler_params=_params("parallel", "arbitrary"),
        name="ffn",
    )(x2d, x2d, gpre, w_up, w_up, conv_taps, w_down, gpost)


def _ple_kernel(x_ref, p_ref, wple_ref, wgate_ref, gpost_ref, o_ref):
    x = x_ref[...]
    e = jnp.dot(p_ref[...].astype(BF16), wple_ref[...], preferred_element_type=F32)
    gate = jnp.dot(x.astype(BF16), wgate_ref[...], preferred_element_type=F32)
    o_ref[...] = x + _rmsnorm(jax.nn.sigmoid(gate) * e, gpost_ref[...])


def _ple(x2d, p2d, w_ple, w_gate, gpost, *, tm=512):
    t, d = x2d.shape
    pd = p2d.shape[1]
    return pl.pallas_call(
        _ple_kernel,
        out_shape=jax.ShapeDtypeStruct((t, d), F32),
        grid=(t // tm,),
        in_specs=[
            pl.BlockSpec((tm, d), lambda i: (i, 0)),
            pl.BlockSpec((tm, pd), lambda i: (i, 0)),
            pl.BlockSpec((pd, d), lambda i: (0, 0)),
            pl.BlockSpec((d, d), lambda i: (0, 0)),
            pl.BlockSpec((1, d), lambda i: (0, 0)),
        ],
        out_specs=pl.BlockSpec((tm, d), lambda i: (i, 0)),
        compiler_params=_params("parallel"),
        name="ple",
    )(x2d, p2d, w_ple, w_gate, gpost)


def kernel(x, p, norm_mix_pre, w_in, w_attn_branch, w_pool_group, pool_scale, w_pool_branch, w_out,
           norm_mix_post, norm_ffn_pre, w_up, conv_w, conv_b, w_down, norm_ffn_post, w_ple, w_ple_gate,
           norm_ple_post):
    batch, seq, d = x.shape
    depth = w_in.shape[0]
    x2d = x.reshape(batch * seq, d)
    row = lambda a: a.reshape(1, -1)
    for l in range(depth):
        proj, (w_up_b,) = _in_proj(x2d, row(norm_mix_pre[l]), w_in[l].astype(BF16), [w_up[l]])
        wpg = w_pool_group[l]
        attn, (wab_b, wpg_b, wpb_b, wout_b, w_down_b) = _attention(
            proj, batch, seq,
            [w_attn_branch[l], wpg.reshape(-1, wpg.shape[-1]), w_pool_branch[l], w_out[l], w_down[l]])
        x2d, (w_gate_b, w_ple_b) = _mix(
            attn, proj, x2d, wab_b, wpg_b.reshape(wpg.shape), row(pool_scale[l]), wpb_b, wout_b,
            row(norm_mix_post[l]), seq, [w_ple_gate[l], w_ple[l]])
        x2d = _ffn(x2d, row(norm_ffn_pre[l]), w_up_b, conv_w[l], row(conv_b[l]), w_down_b,
                   row(norm_ffn_post[l]), seq)
        x2d = _ple(x2d, p[l].reshape(batch * seq, -1), w_ple_b, w_gate_b, row(norm_ple_post[l]))
    return x2d.reshape(batch, seq, d)
```

```python
import functools
import math

import jax
import jax.numpy as jnp
from jax import lax
from jax.experimental import pallas as pl
from jax.experimental.pallas import tpu as pltpu

N_ATTN_HEADS = 8
HEAD_DIM = 128
ATTN_WIDTH = N_ATTN_HEADS * HEAD_DIM
POOL_WINDOWS = (2, 4, 8, 16)
POOL_GROUP_WIDTH = 256
POOL_WIDTH = len(POOL_WINDOWS) * POOL_GROUP_WIDTH
CONV_WIDTH = 3
EPS = 1e-6

HALO = 16

Q_SCALE = HEAD_DIM ** -0.5 * math.log2(math.e)
F32_POW2_UNDERFLOW = 150.0

VMEM_LIMIT_BYTES = 56 * 1024 * 1024

LANES = 128
BF16_SUBLANES = 16

F32 = jnp.float32
BF16 = jnp.bfloat16


def _rmsnorm(x, gain):
    return x * lax.rsqrt(jnp.mean(x * x, axis=-1, keepdims=True) + EPS) * gain


def _rms_scale(src_ref, rinv_ref):
    @pl.when(pl.program_id(0) >= 0)
    def _():
        y = src_ref[...]
        ms = jnp.mean(y * y, axis=-1, keepdims=True)
        rinv_ref[...] = jnp.broadcast_to(lax.rsqrt(ms + EPS), rinv_ref.shape)

    return jnp.concatenate([rinv_ref[...]] * (src_ref.shape[1] // rinv_ref.shape[1]), axis=1)


def _params(*semantics):
    return pltpu.CompilerParams(dimension_semantics=semantics, vmem_limit_bytes=VMEM_LIMIT_BYTES)


def _side_cast_specs(weights, grid, step_of):
    n_steps = math.prod(grid)
    in_specs, out_specs, out_shapes = [], [], []
    for w in weights:
        rows, cols = w.shape
        n_blocks = math.gcd(n_steps, rows // BF16_SUBLANES)
        stride = n_steps // n_blocks
        index = lambda *g, stride=stride: (step_of(*g) // stride, 0)
        in_specs.append(pl.BlockSpec((rows // n_blocks, cols), index))
        out_specs.append(pl.BlockSpec((rows // n_blocks, cols), index))
        out_shapes.append(jax.ShapeDtypeStruct((rows, cols), BF16))
    return in_specs, out_specs, out_shapes


def _cast_sides(src_refs, dst_refs):
    for src, dst in zip(src_refs, dst_refs):
        dst[...] = src[...].astype(BF16)


def _in_proj_kernel(x_ref, g_ref, w_ref, *refs, tn, n_side):
    side_src, (o_ref, *side_dst), h_ref = refs[:n_side], refs[n_side:2 * n_side + 1], refs[-1]
    j = pl.program_id(1)
    _cast_sides(side_src, side_dst)

    @pl.when(j == 0)
    def _():
        h_ref[...] = _rmsnorm(x_ref[...], g_ref[...]).astype(BF16)

    col = j * tn + lax.broadcasted_iota(jnp.int32, (1, tn), 1)
    col_scale = jnp.where(col < ATTN_WIDTH, Q_SCALE, 1.0)
    y = jnp.dot(h_ref[...], w_ref[...], preferred_element_type=F32)
    o_ref[...] = (y * col_scale).astype(o_ref.dtype)


def _in_proj(x2d, gain, w_bf16, side, *, tm=1024, tn=2048):
    t, d = x2d.shape
    n = w_bf16.shape[1]
    grid = (t // tm, n // tn)
    side_in, side_out, side_shapes = _side_cast_specs(side, grid, lambda i, j: i * grid[1] + j)
    proj, *side_bf16 = pl.pallas_call(
        functools.partial(_in_proj_kernel, tn=tn, n_side=len(side)),
        out_shape=[jax.ShapeDtypeStruct((t, n), BF16)] + side_shapes,
        grid=grid,
        in_specs=[
            pl.BlockSpec((tm, d), lambda i, j: (i, 0)),
            pl.BlockSpec((1, d), lambda i, j: (0, 0)),
            pl.BlockSpec((d, tn), lambda i, j: (0, j)),
        ] + side_in,
        out_specs=[pl.BlockSpec((tm, tn), lambda i, j: (i, j))] + side_out,
        scratch_shapes=[pltpu.VMEM((tm, d), BF16)],
        compiler_params=_params("arbitrary", "arbitrary"),
        name="in_proj",
    )(x2d, gain, w_bf16, *side)
    return proj, side_bf16


def _attn_kernel(q_ref, k_ref, v_ref, *refs, tq, tk, heads, n_side):
    side_src, (o_ref, *side_dst), (acc_ref, carry_ref) = refs[:n_side], refs[n_side:2 * n_side + 1], refs[-2:]
    qi = pl.program_id(2)
    _cast_sides(side_src, side_dst)

    r = lax.broadcasted_iota(jnp.int32, (2 * tk, 2 * tk), 0)
    c = lax.broadcasted_iota(jnp.int32, (2 * tk, 2 * tk), 1)
    r = jnp.where(r >= tk, r - tk, r)
    sum_mat = jnp.where(c >= tk, 1.0, jnp.where(r > c, 1.0, 0.0)).astype(BF16)

    def block_sums(nl):
        hi = nl.astype(BF16)
        lo = (nl - hi.astype(F32)).astype(BF16)
        return jnp.dot(jnp.concatenate([hi, lo], axis=1), sum_mat, preferred_element_type=F32)

    def logits(q, k):
        z = lax.dot_general(q, k, (((1,), (1,)), ((), ())), preferred_element_type=F32)
        nl = jnp.maximum(z, 0.0) + jnp.log2(1.0 + jnp.exp2(-jnp.abs(z)))
        return nl, z - nl

    def diagonal(kb):
        ks = pl.multiple_of((kb - 1) * tk, tk)
        row = lax.broadcasted_iota(jnp.int32, (tq, tk), 0)
        col = lax.broadcasted_iota(jnp.int32, (tq, tk), 1)
        causal_lo = col < row
        causal_hi = causal_lo[:tk]
        for h in range(heads):
            dcol = slice(h * HEAD_DIM, (h + 1) * HEAD_DIM)
            ccol = slice(h * tk, (h + 1) * tk)
            nl_lo, w_lo = logits(q_ref[:, dcol], k_ref[pl.ds(ks, tk), dcol])
            nl_hi, w_hi = logits(q_ref[tk:, dcol], k_ref[pl.ds(ks + tk, tk), dcol])
            sums_lo = block_sums(jnp.where(causal_lo, nl_lo, 0.0))
            sums_hi = block_sums(jnp.where(causal_hi, nl_hi, 0.0))
            total_hi = jnp.concatenate([jnp.zeros((tk, tk), F32), sums_hi[:, tk:]], axis=0)
            a_lo = jnp.where(causal_lo, jnp.exp2(w_lo - sums_lo[:, :tk] - total_hi), 0.0)
            a_hi = jnp.where(causal_hi, jnp.exp2(w_hi - sums_hi[:, :tk]), 0.0)
            acc_lo = jnp.dot(a_lo.astype(BF16), v_ref[pl.ds(ks, tk), dcol], preferred_element_type=F32)
            acc_hi = jnp.dot(a_hi.astype(BF16), v_ref[pl.ds(ks + tk, tk), dcol], preferred_element_type=F32)
            acc_ref[:tk, dcol] = acc_lo[:tk]
            acc_ref[tk:, dcol] = acc_lo[tk:] + acc_hi
            carry_ref[:, ccol] = total_hi + sums_lo[:, tk:]

    def pair(kb):
        ks = pl.multiple_of((kb - 1) * tk, tk)
        for h in range(heads):
            dcol = slice(h * HEAD_DIM, (h + 1) * HEAD_DIM)
            ccol = slice(h * tk, (h + 1) * tk)
            nl, w = logits(q_ref[:, dcol], k_ref[pl.ds(ks, 2 * tk), dcol])
            sums_hi = block_sums(nl[:, tk:])
            sums_lo = block_sums(nl[:, :tk])
            carry = carry_ref[:, ccol]
            carry_mid = carry + sums_hi[:, tk:]
            later = jnp.concatenate([sums_lo[:, :tk] + carry_mid, sums_hi[:, :tk] + carry], axis=1)
            a = jnp.exp2(w - later)
            acc_ref[:, dcol] += jnp.dot(a.astype(BF16), v_ref[pl.ds(ks, 2 * tk), dcol],
                                        preferred_element_type=F32)
            carry_ref[:, ccol] = carry_mid + sums_lo[:, tk:]

    assert tq == 2 * tk
    diag_kb = 2 * qi + 1

    @pl.when(qi == 0)
    def _():
        diagonal(diag_kb)

    @pl.when(qi > 0)
    def _():
        diagonal(diag_kb)
        pair(diag_kb - 2)

    def still_live():
        return (jnp.min(carry_ref[...]) <= F32_POW2_UNDERFLOW).astype(jnp.int32)

    def cond(state):
        kb, live = state
        return jnp.logical_and(kb >= 1, live > 0)

    def body(state):
        kb, _ = state
        pair(kb)
        return kb - 2, still_live()

    lax.while_loop(cond, body, (diag_kb - 4, still_live()))
    o_ref[...] = acc_ref[...].astype(o_ref.dtype)


def _attention(proj, batch, seq, side, *, tq=256, tk=128, heads=8):
    t = proj.shape[0]
    nq = seq // tq
    hg = N_ATTN_HEADS // heads
    width = heads * HEAD_DIM
    grid = (batch, hg, nq)
    side_in, side_out, side_shapes = _side_cast_specs(side, grid, lambda b, g, i: (b * hg + g) * nq + i)
    kern = functools.partial(_attn_kernel, tq=tq, tk=tk, heads=heads, n_side=len(side))
    attn, *side_bf16 = pl.pallas_call(
        kern,
        out_shape=[jax.ShapeDtypeStruct((t, ATTN_WIDTH), BF16)] + side_shapes,
        grid=grid,
        in_specs=[
            pl.BlockSpec((tq, width), lambda b, g, i: (b * nq + i, g)),
            pl.BlockSpec((seq, width), lambda b, g, i: (b, hg + g)),
            pl.BlockSpec((seq, width), lambda b, g, i: (b, 2 * hg + g)),
        ] + side_in,
        out_specs=[pl.BlockSpec((tq, width), lambda b, g, i: (b * nq + i, g))] + side_out,
        scratch_shapes=[pltpu.VMEM((tq, width), F32), pltpu.VMEM((tq, heads * tk), F32)],
        compiler_params=_params("arbitrary", "arbitrary", "arbitrary"),
        name="attn",
    )(proj, proj, proj, *side)
    return attn, side_bf16


def _mix_kernel(attn_ref, u_ref, uh_ref, ga_ref, gp_ref, x_ref, wab_ref, wpg_ref, ps_ref, wpb_ref,
                wout_ref, gpost_ref, *refs, tm, seq, n_side):
    side_src, (o_ref, *side_dst) = refs[:n_side], refs[n_side:]
    i = pl.program_id(0)
    _cast_sides(side_src, side_dst)
    seq_pos0 = (i * tm) % seq
    halo = jnp.where(seq_pos0 == 0, 0.0, uh_ref[...].astype(F32))
    ucat = jnp.concatenate([halo, u_ref[...].astype(F32)], axis=0)

    pos = seq_pos0 + lax.broadcasted_iota(jnp.int32, (tm, 1), 0)
    parts = []
    y_attn_parts = []
    n_groups = len(POOL_WINDOWS)
    d = o_ref.shape[1]
    for g, w in enumerate(POOL_WINDOWS):
        acols = slice(g * d // n_groups, (g + 1) * d // n_groups)
        y_attn_parts.append(jnp.dot(attn_ref[...], wab_ref[:, acols], preferred_element_type=F32))
        cols = slice(g * POOL_GROUP_WIDTH, (g + 1) * POOL_GROUP_WIDTH)
        s = ucat[:, cols]
        k = 1
        while k < w:
            s = s + pltpu.roll(s, k, axis=0)
            k *= 2
        count = jnp.minimum(pos + 1, w).astype(F32)
        pooled = s[HALO:] / count - ucat[HALO:, cols]
        pg = jnp.dot(pooled.astype(BF16), wpg_ref[g], preferred_element_type=F32)
        parts.append((pg * ps_ref[:, cols]).astype(BF16))
    y_attn = jnp.concatenate(y_attn_parts, axis=1)
    y_pool = jnp.dot(jnp.concatenate(parts, axis=1), wpb_ref[...], preferred_element_type=F32)
    mixed = (jax.nn.sigmoid(ga_ref[...].astype(F32)) * y_attn
             + jax.nn.sigmoid(gp_ref[...].astype(F32)) * y_pool)
    y = jnp.dot(mixed.astype(BF16), wout_ref[...], preferred_element_type=F32)
    o_ref[...] = x_ref[...] + _rmsnorm(y, gpost_ref[...])


def _mix(attn, proj, x2d, wab, wpg, pool_scale, wpb, wout, gpost, seq, side, *, tm=512):
    t, d = x2d.shape
    assert all(w & (w - 1) == 0 and w <= HALO for w in POOL_WINDOWS)
    u_col = 3 * ATTN_WIDTH // POOL_WIDTH
    g_col = (3 * ATTN_WIDTH + POOL_WIDTH) // d
    const = lambda *shape: pl.BlockSpec(shape, lambda i: (0,) * len(shape), pipeline_mode=pl.Buffered(1))
    grid = (t // tm,)
    side_in, side_out, side_shapes = _side_cast_specs(side, grid, lambda i: i)
    kern = functools.partial(_mix_kernel, tm=tm, seq=seq, n_side=len(side))
    x1, *side_bf16 = pl.pallas_call(
        kern,
        out_shape=[jax.ShapeDtypeStruct((t, d), F32)] + side_shapes,
        grid=grid,
        in_specs=[
            pl.BlockSpec((tm, ATTN_WIDTH), lambda i: (i, 0)),
            pl.BlockSpec((tm, POOL_WIDTH), lambda i: (i, u_col)),
            pl.BlockSpec((HALO, POOL_WIDTH), lambda i: (jnp.maximum(i * (tm // HALO) - 1, 0), u_col)),
            pl.BlockSpec((tm, d), lambda i: (i, g_col)),
            pl.BlockSpec((tm, d), lambda i: (i, g_col + 1)),
            pl.BlockSpec((tm, d), lambda i: (i, 0)),
            const(ATTN_WIDTH, d),
            const(len(POOL_WINDOWS), POOL_GROUP_WIDTH, POOL_GROUP_WIDTH),
            const(1, POOL_WIDTH),
            const(POOL_WIDTH, d),
            const(d, d),
            const(1, d),
        ] + side_in,
        out_specs=[pl.BlockSpec((tm, d), lambda i: (i, 0))] + side_out,
        compiler_params=_params("arbitrary"),
        name="mix",
    )(attn, proj, proj, proj, proj, x2d, wab, wpg, pool_scale, wpb, wout, gpost, *side)
    return x1, side_bf16


def _gelu_tanh(x):
    return 0.5 * x * (1.0 + jnp.tanh(0.7978845608028654 * (x + 0.044715 * (x * x * x))))


def _ffn_kernel(x_ref, xh_ref, gpre_ref, wg_ref, wv_ref, conv_ref, wd_ref, gpost_ref, o_ref, h_ref, up_ref,
                rinv_ref, *, tm, seq, n_down):
    i = pl.program_id(0)
    j = pl.program_id(1)
    nf = pl.num_programs(1)

    @pl.when(j == 0)
    def _():
        hh = _rmsnorm(xh_ref[...], gpre_ref[...])
        h_ref[0:HALO, :] = jnp.where((i * tm) % seq == 0, 0.0, hh).astype(BF16)
        h_ref[HALO:, :] = _rmsnorm(x_ref[...], gpre_ref[...]).astype(BF16)
        o_ref[...] = jnp.zeros_like(o_ref)

    h = h_ref[...]

    def conv(slot, w_ref, chunk):
        up_ref[slot] = jnp.dot(h, w_ref[...], preferred_element_type=F32)
        taps = conv_ref[chunk]
        out = taps[CONV_WIDTH:CONV_WIDTH + 1, :]
        for tap in range(CONV_WIDTH):
            off = HALO - (CONV_WIDTH - 1 - tap)
            out = out + up_ref[slot, off:off + tm, :] * taps[tap:tap + 1, :]
        return out

    gate = conv(0, wg_ref, j)
    val = conv(1, wv_ref, nf + j)
    act = (_gelu_tanh(gate) * val).astype(BF16)
    tc = wd_ref.shape[0] // n_down
    for c in range(n_down):
        rows = slice(c * tc, (c + 1) * tc)
        o_ref[...] += jnp.dot(act[:, rows], wd_ref[rows, :], preferred_element_type=F32)

    @pl.when(j == pl.num_programs(1) - 1)
    def _():
        scale = _rms_scale(o_ref, rinv_ref)
        o_ref[...] = x_ref[...] + o_ref[...] * scale * gpost_ref[...]


def _ffn(x2d, gpre, w_up, conv_w, conv_b, w_down, gpost, seq, *, tm=512, tf=512, n_down=2):
    t, d = x2d.shape
    d_ff = w_down.shape[0]
    nf = d_ff // tf
    kern = functools.partial(_ffn_kernel, tm=tm, seq=seq, n_down=n_down)
    conv_taps = jnp.concatenate([conv_w, conv_b], axis=0).reshape(CONV_WIDTH + 1, 2 * nf, tf)
    conv_taps = conv_taps.transpose(1, 0, 2)
    return pl.pallas_call(
        kern,
        out_shape=jax.ShapeDtypeStruct((t, d), F32),
        grid=(t // tm, nf),
        in_specs=[
            pl.BlockSpec((tm, d), lambda i, j: (i, 0)),
            pl.BlockSpec((HALO, d), lambda i, j: (jnp.maximum(i * (tm // HALO) - 1, 0), 0)),
            pl.BlockSpec((1, d), lambda i, j: (0, 0)),
            pl.BlockSpec((d, tf), lambda i, j: (0, j)),
            pl.BlockSpec((d, tf), lambda i, j: (0, nf + j)),
            pl.BlockSpec((2 * nf, CONV_WIDTH + 1, tf), lambda i, j: (0, 0, 0)),
            pl.BlockSpec((tf, d), lambda i, j: (j, 0)),
            pl.BlockSpec((1, d), lambda i, j: (0, 0)),
        ],
        out_specs=pl.BlockSpec((tm, d), lambda i, j: (i, 0)),
        scratch_shapes=[pltpu.VMEM((HALO + tm, d), BF16),
                        pltpu.VMEM((2, HALO + tm, tf), F32),
                        pltpu.VMEM((tm, LANES), F32)],
        compiler_params=_params("parallel", "arbitrary"),
        name="ffn",
    )(x2d, x2d, gpre, w_up, w_up, conv_taps, w_down, gpost)


def _ple_kernel(x_ref, p_ref, wple_ref, wgate_ref, gpost_ref, o_ref):
    x = x_ref[...]
    e = jnp.dot(p_ref[...].astype(BF16), wple_ref[...], preferred_element_type=F32)
    gate = jnp.dot(x.astype(BF16), wgate_ref[...], preferred_element_type=F32)
    o_ref[...] = x + _rmsnorm(jax.nn.sigmoid(gate) * e, gpost_ref[...])


def _ple(x2d, p2d, w_ple, w_gate, gpost, *, tm=512):
    t, d = x2d.shape
    pd = p2d.shape[1]
    return pl.pallas_call(
        _ple_kernel,
        out_shape=jax.ShapeDtypeStruct((t, d), F32),
        grid=(t // tm,),
        in_specs=[
            pl.BlockSpec((tm, d), lambda i: (i, 0)),
            pl.BlockSpec((tm, pd), lambda i: (i, 0)),
            pl.BlockSpec((pd, d), lambda i: (0, 0)),
            pl.BlockSpec((d, d), lambda i: (0, 0)),
            pl.BlockSpec((1, d), lambda i: (0, 0)),
        ],
        out_specs=pl.BlockSpec((tm, d), lambda i: (i, 0)),
        compiler_params=_params("parallel"),
        name="ple",
    )(x2d, p2d, w_ple, w_gate, gpost)


def kernel(x, p, norm_mix_pre, w_in, w_attn_branch, w_pool_group, pool_scale, w_pool_branch, w_out,
           norm_mix_post, norm_ffn_pre, w_up, conv_w, conv_b, w_down, norm_ffn_post, w_ple, w_ple_gate,
           norm_ple_post):
    batch, seq, d = x.shape
    depth = w_in.shape[0]
    x2d = x.reshape(batch * seq, d)
    row = lambda a: a.reshape(1, -1)
    for l in range(depth):
        proj, (w_up_b,) = _in_proj(x2d, row(norm_mix_pre[l]), w_in[l].astype(BF16), [w_up[l]])
        wpg = w_pool_group[l]
        attn, (wab_b, wpg_b, wpb_b, wout_b, w_down_b) = _attention(
            proj, batch, seq,
            [w_attn_branch[l], wpg.reshape(-1, wpg.shape[-1]), w_pool_branch[l], w_out[l], w_down[l]])
        x2d, (w_gate_b, w_ple_b) = _mix(
            attn, proj, x2d, wab_b, wpg_b.reshape(wpg.shape), row(pool_scale[l]), wpb_b, wout_b,
            row(norm_mix_post[l]), seq, [w_ple_gate[l], w_ple[l]])
        x2d = _ffn(x2d, row(norm_ffn_pre[l]), w_up_b, conv_w[l], row(conv_b[l]), w_down_b,
                   row(norm_ffn_post[l]), seq)
        x2d = _ple(x2d, p[l].reshape(batch * seq, -1), w_ple_b, w_gate_b, row(norm_ple_post[l]))
    return x2d.reshape(batch, seq, d)
```

```python
import functools
import math

import jax
import jax.numpy as jnp
from jax import lax
from jax.experimental import pallas as pl
from jax.experimental.pallas import tpu as pltpu

N_ATTN_HEADS = 8
HEAD_DIM = 128
ATTN_WIDTH = N_ATTN_HEADS * HEAD_DIM
POOL_WINDOWS = (2, 4, 8, 16)
POOL_GROUP_WIDTH = 256
POOL_WIDTH = len(POOL_WINDOWS) * POOL_GROUP_WIDTH
CONV_WIDTH = 3
EPS = 1e-6

LANES = 128
F32_SUBLANES = 8
BF16_SUBLANES = 16

HALO = BF16_SUBLANES

Q_SCALE = HEAD_DIM ** -0.5 * math.log2(math.e)
F32_POW2_UNDERFLOW = 150.0

VMEM_LIMIT_BYTES = 56 * 1024 * 1024

F32 = jnp.float32
BF16 = jnp.bfloat16


def _rmsnorm(x, gain):
    return x * lax.rsqrt(jnp.mean(x * x, axis=-1, keepdims=True) + EPS) * gain


def _rms_scale(src_ref, rinv_ref):
    @pl.when(pl.program_id(0) >= 0)
    def _():
        y = src_ref[...]
        ms = jnp.mean(y * y, axis=-1, keepdims=True)
        rinv_ref[...] = jnp.broadcast_to(lax.rsqrt(ms + EPS), rinv_ref.shape)

    return jnp.concatenate([rinv_ref[...]] * (src_ref.shape[1] // rinv_ref.shape[1]), axis=1)


def _params(*semantics):
    return pltpu.CompilerParams(dimension_semantics=semantics, vmem_limit_bytes=VMEM_LIMIT_BYTES)


def _side_cast_specs(weights, grid, step_of):
    n_steps = math.prod(grid)
    in_specs, out_specs, out_shapes = [], [], []
    for w in weights:
        rows, cols = w.shape
        n_blocks = math.gcd(n_steps, rows // BF16_SUBLANES)
        stride = n_steps // n_blocks
        index = lambda *g, stride=stride: (step_of(*g) // stride, 0)
        in_specs.append(pl.BlockSpec((rows // n_blocks, cols), index))
        out_specs.append(pl.BlockSpec((rows // n_blocks, cols), index))
        out_shapes.append(jax.ShapeDtypeStruct((rows, cols), BF16))
    return in_specs, out_specs, out_shapes


def _cast_sides(src_refs, dst_refs):
    for src, dst in zip(src_refs, dst_refs):
        dst[...] = src[...].astype(BF16)


def _in_proj_kernel(x_ref, g_ref, w_ref, *refs, tn, n_side):
    side_src, (o_ref, *side_dst), h_ref = refs[:n_side], refs[n_side:2 * n_side + 1], refs[-1]
    j = pl.program_id(1)
    _cast_sides(side_src, side_dst)

    @pl.when(j == 0)
    def _():
        h_ref[...] = _rmsnorm(x_ref[...], g_ref[...]).astype(BF16)

    col = j * tn + lax.broadcasted_iota(jnp.int32, (1, tn), 1)
    col_scale = jnp.where(col < ATTN_WIDTH, Q_SCALE, 1.0)
    y = jnp.dot(h_ref[...], w_ref[...], preferred_element_type=F32)
    o_ref[...] = (y * col_scale).astype(o_ref.dtype)


def _in_proj(x2d, gain, w_bf16, side, *, tm=1024, tn=2048):
    t, d = x2d.shape
    n = w_bf16.shape[1]
    assert t % tm == 0 and n % tn == 0
    grid = (t // tm, n // tn)
    side_in, side_out, side_shapes = _side_cast_specs(side, grid, lambda i, j: i * grid[1] + j)
    proj, *side_bf16 = pl.pallas_call(
        functools.partial(_in_proj_kernel, tn=tn, n_side=len(side)),
        out_shape=[jax.ShapeDtypeStruct((t, n), BF16)] + side_shapes,
        grid=grid,
        in_specs=[
            pl.BlockSpec((tm, d), lambda i, j: (i, 0)),
            pl.BlockSpec((1, d), lambda i, j: (0, 0)),
            pl.BlockSpec((d, tn), lambda i, j: (0, j)),
        ] + side_in,
        out_specs=[pl.BlockSpec((tm, tn), lambda i, j: (i, j))] + side_out,
        scratch_shapes=[pltpu.VMEM((tm, d), BF16)],
        compiler_params=_params("arbitrary", "arbitrary"),
        name="in_proj",
    )(x2d, gain, w_bf16, *side)
    return proj, side_bf16


def _attn_kernel(q_ref, k_ref, v_ref, *refs, tq, tk, heads, n_side):
    side_src, (o_ref, *side_dst), (acc_ref, carry_ref) = refs[:n_side], refs[n_side:2 * n_side + 1], refs[-2:]
    qi = pl.program_id(2)
    _cast_sides(side_src, side_dst)

    r = lax.broadcasted_iota(jnp.int32, (2 * tk, 2 * tk), 0)
    c = lax.broadcasted_iota(jnp.int32, (2 * tk, 2 * tk), 1)
    r = jnp.where(r >= tk, r - tk, r)
    sum_mat = jnp.where(c >= tk, 1.0, jnp.where(r > c, 1.0, 0.0)).astype(BF16)

    def block_sums(nl):
        hi = nl.astype(BF16)
        lo = (nl - hi.astype(F32)).astype(BF16)
        return jnp.dot(jnp.concatenate([hi, lo], axis=1), sum_mat, preferred_element_type=F32)

    def logits(q, k):
        z = lax.dot_general(q, k, (((1,), (1,)), ((), ())), preferred_element_type=F32)
        nl = jnp.maximum(z, 0.0) + jnp.log2(1.0 + jnp.exp2(-jnp.abs(z)))
        return nl, z - nl

    def diagonal(kb):
        ks = pl.multiple_of((kb - 1) * tk, tk)
        row = lax.broadcasted_iota(jnp.int32, (tq, tk), 0)
        col = lax.broadcasted_iota(jnp.int32, (tq, tk), 1)
        causal_lo = col < row
        causal_hi = causal_lo[:tk]
        for h in range(heads):
            dcol = slice(h * HEAD_DIM, (h + 1) * HEAD_DIM)
            ccol = slice(h * tk, (h + 1) * tk)
            nl_lo, w_lo = logits(q_ref[:, dcol], k_ref[pl.ds(ks, tk), dcol])
            nl_hi, w_hi = logits(q_ref[tk:, dcol], k_ref[pl.ds(ks + tk, tk), dcol])
            sums_lo = block_sums(jnp.where(causal_lo, nl_lo, 0.0))
            sums_hi = block_sums(jnp.where(causal_hi, nl_hi, 0.0))
            total_hi = jnp.concatenate([jnp.zeros((tk, tk), F32), sums_hi[:, tk:]], axis=0)
            a_lo = jnp.where(causal_lo, jnp.exp2(w_lo - sums_lo[:, :tk] - total_hi), 0.0)
            a_hi = jnp.where(causal_hi, jnp.exp2(w_hi - sums_hi[:, :tk]), 0.0)
            acc_lo = jnp.dot(a_lo.astype(BF16), v_ref[pl.ds(ks, tk), dcol], preferred_element_type=F32)
            acc_hi = jnp.dot(a_hi.astype(BF16), v_ref[pl.ds(ks + tk, tk), dcol], preferred_element_type=F32)
            acc_ref[:tk, dcol] = acc_lo[:tk]
            acc_ref[tk:, dcol] = acc_lo[tk:] + acc_hi
            carry_ref[:, ccol] = total_hi + sums_lo[:, tk:]

    def pair(kb):
        ks = pl.multiple_of((kb - 1) * tk, tk)
        for h in range(heads):
            dcol = slice(h * HEAD_DIM, (h + 1) * HEAD_DIM)
            ccol = slice(h * tk, (h + 1) * tk)
            nl, w = logits(q_ref[:, dcol], k_ref[pl.ds(ks, 2 * tk), dcol])
            sums_hi = block_sums(nl[:, tk:])
            sums_lo = block_sums(nl[:, :tk])
            carry = carry_ref[:, ccol]
            carry_mid = carry + sums_hi[:, tk:]
            later = jnp.concatenate([sums_lo[:, :tk] + carry_mid, sums_hi[:, :tk] + carry], axis=1)
            a = jnp.exp2(w - later)
            acc_ref[:, dcol] += jnp.dot(a.astype(BF16), v_ref[pl.ds(ks, 2 * tk), dcol],
                                        preferred_element_type=F32)
            carry_ref[:, ccol] = carry_mid + sums_lo[:, tk:]

    assert tq == 2 * tk
    diag_kb = 2 * qi + 1

    @pl.when(qi == 0)
    def _():
        diagonal(diag_kb)

    @pl.when(qi > 0)
    def _():
        diagonal(diag_kb)
        pair(diag_kb - 2)

    def still_live():
        c = carry_ref[...]
        while c.shape[1] > LANES:
            half = c.shape[1] // 2
            c = jnp.minimum(c[:, :half], c[:, half:])
        while c.shape[0] > F32_SUBLANES:
            half = c.shape[0] // 2
            c = jnp.minimum(c[:half], c[half:])
        return (jnp.min(c) <= F32_POW2_UNDERFLOW).astype(jnp.int32)

    def cond(state):
        kb, live = state
        return jnp.logical_and(kb >= 1, live > 0)

    def body(state):
        kb, _ = state
        pair(kb)
        return kb - 2, still_live()

    lax.while_loop(cond, body, (diag_kb - 4, still_live()))
    o_ref[...] = acc_ref[...].astype(o_ref.dtype)


def _attention(proj, batch, seq, side, *, tq=256, tk=128, heads=8):
    t = proj.shape[0]
    assert t == batch * seq and seq % tq == 0 and N_ATTN_HEADS % heads == 0
    nq = seq // tq
    hg = N_ATTN_HEADS // heads
    width = heads * HEAD_DIM
    grid = (batch, hg, nq)
    side_in, side_out, side_shapes = _side_cast_specs(side, grid, lambda b, g, i: (b * hg + g) * nq + i)
    kern = functools.partial(_attn_kernel, tq=tq, tk=tk, heads=heads, n_side=len(side))
    attn, *side_bf16 = pl.pallas_call(
        kern,
        out_shape=[jax.ShapeDtypeStruct((t, ATTN_WIDTH), BF16)] + side_shapes,
        grid=grid,
        in_specs=[
            pl.BlockSpec((tq, width), lambda b, g, i: (b * nq + i, g)),
            pl.BlockSpec((seq, width), lambda b, g, i: (b, hg + g)),
            pl.BlockSpec((seq, width), lambda b, g, i: (b, 2 * hg + g)),
        ] + side_in,
        out_specs=[pl.BlockSpec((tq, width), lambda b, g, i: (b * nq + i, g))] + side_out,
        scratch_shapes=[pltpu.VMEM((tq, width), F32), pltpu.VMEM((tq, heads * tk), F32)],
        compiler_params=_params("arbitrary", "arbitrary", "arbitrary"),
        name="attn",
    )(proj, proj, proj, *side)
    return attn, side_bf16


def _mix_kernel(attn_ref, u_ref, uh_ref, ga_ref, gp_ref, x_ref, wab_ref, wpg_ref, ps_ref, wpb_ref,
                wout_ref, gpost_ref, *refs, tm, seq, n_side):
    side_src, (o_ref, *side_dst) = refs[:n_side], refs[n_side:]
    i = pl.program_id(0)
    _cast_sides(side_src, side_dst)
    seq_pos0 = (i * tm) % seq
    halo = jnp.where(seq_pos0 == 0, 0.0, uh_ref[...].astype(F32))
    ucat = jnp.concatenate([halo, u_ref[...].astype(F32)], axis=0)

    pos = seq_pos0 + lax.broadcasted_iota(jnp.int32, (tm, 1), 0)
    parts = []
    y_attn_parts = []
    n_groups = len(POOL_WINDOWS)
    d = o_ref.shape[1]
    for g, w in enumerate(POOL_WINDOWS):
        acols = slice(g * d // n_groups, (g + 1) * d // n_groups)
        y_attn_parts.append(jnp.dot(attn_ref[...], wab_ref[:, acols], preferred_element_type=F32))
        cols = slice(g * POOL_GROUP_WIDTH, (g + 1) * POOL_GROUP_WIDTH)
        s = ucat[:, cols]
        k = 1
        while k < w:
            s = s + pltpu.roll(s, k, axis=0)
            k *= 2
        count = jnp.minimum(pos + 1, w).astype(F32)
        pooled = s[HALO:] / count - ucat[HALO:, cols]
        pg = jnp.dot(pooled.astype(BF16), wpg_ref[g], preferred_element_type=F32)
        parts.append((pg * ps_ref[:, cols]).astype(BF16))
    y_attn = jnp.concatenate(y_attn_parts, axis=1)
    y_pool = jnp.dot(jnp.concatenate(parts, axis=1), wpb_ref[...], preferred_element_type=F32)
    mixed = (jax.nn.sigmoid(ga_ref[...].astype(F32)) * y_attn
             + jax.nn.sigmoid(gp_ref[...].astype(F32)) * y_pool)
    y = jnp.dot(mixed.astype(BF16), wout_ref[...], preferred_element_type=F32)
    o_ref[...] = x_ref[...] + _rmsnorm(y, gpost_ref[...])


def _mix(attn, proj, x2d, wab, wpg, pool_scale, wpb, wout, gpost, seq, side, *, tm=512):
    t, d = x2d.shape
    assert all(w & (w - 1) == 0 and w <= HALO for w in POOL_WINDOWS)
    assert seq % tm == 0 and tm % HALO == 0
    u_col = 3 * ATTN_WIDTH // POOL_WIDTH
    g_col = (3 * ATTN_WIDTH + POOL_WIDTH) // d
    const = lambda *shape: pl.BlockSpec(shape, lambda i: (0,) * len(shape), pipeline_mode=pl.Buffered(1))
    grid = (t // tm,)
    side_in, side_out, side_shapes = _side_cast_specs(side, grid, lambda i: i)
    kern = functools.partial(_mix_kernel, tm=tm, seq=seq, n_side=len(side))
    x1, *side_bf16 = pl.pallas_call(
        kern,
        out_shape=[jax.ShapeDtypeStruct((t, d), F32)] + side_shapes,
        grid=grid,
        in_specs=[
            pl.BlockSpec((tm, ATTN_WIDTH), lambda i: (i, 0)),
            pl.BlockSpec((tm, POOL_WIDTH), lambda i: (i, u_col)),
            pl.BlockSpec((HALO, POOL_WIDTH), lambda i: (jnp.maximum(i * (tm // HALO) - 1, 0), u_col)),
            pl.BlockSpec((tm, d), lambda i: (i, g_col)),
            pl.BlockSpec((tm, d), lambda i: (i, g_col + 1)),
            pl.BlockSpec((tm, d), lambda i: (i, 0)),
            const(ATTN_WIDTH, d),
            const(len(POOL_WINDOWS), POOL_GROUP_WIDTH, POOL_GROUP_WIDTH),
            const(1, POOL_WIDTH),
            const(POOL_WIDTH, d),
            const(d, d),
            const(1, d),
        ] + side_in,
        out_specs=[pl.BlockSpec((tm, d), lambda i: (i, 0))] + side_out,
        compiler_params=_params("arbitrary"),
        name="mix",
    )(attn, proj, proj, proj, proj, x2d, wab, wpg, pool_scale, wpb, wout, gpost, *side)
    return x1, side_bf16


def _gelu_tanh(x):
    return 0.5 * x * (1.0 + jnp.tanh(0.7978845608028654 * (x + 0.044715 * (x * x * x))))


def _ffn_kernel(x_ref, xh_ref, gpre_ref, wg_ref, wv_ref, conv_ref, wd_ref, gpost_ref, o_ref, h_ref, up_ref,
                rinv_ref, *, tm, seq, n_down):
    i = pl.program_id(0)
    j = pl.program_id(1)
    nf = pl.num_programs(1)

    @pl.when(j == 0)
    def _():
        hh = _rmsnorm(xh_ref[...], gpre_ref[...])
        h_ref[0:HALO, :] = jnp.where((i * tm) % seq == 0, 0.0, hh).astype(BF16)
        h_ref[HALO:, :] = _rmsnorm(x_ref[...], gpre_ref[...]).astype(BF16)
        o_ref[...] = jnp.zeros_like(o_ref)

    h = h_ref[...]

    def conv(slot, w_ref, chunk):
        up_ref[slot] = jnp.dot(h, w_ref[...], preferred_element_type=F32)
        taps = conv_ref[chunk]
        out = taps[CONV_WIDTH:CONV_WIDTH + 1, :]
        for tap in range(CONV_WIDTH):
            off = HALO - (CONV_WIDTH - 1 - tap)
            out = out + up_ref[slot, off:off + tm, :] * taps[tap:tap + 1, :]
        return out

    gate = conv(0, wg_ref, j)
    val = conv(1, wv_ref, nf + j)
    act = (_gelu_tanh(gate) * val).astype(BF16)
    tc = wd_ref.shape[0] // n_down
    for c in range(n_down):
        rows = slice(c * tc, (c + 1) * tc)
        o_ref[...] += jnp.dot(act[:, rows], wd_ref[rows, :], preferred_element_type=F32)

    @pl.when(j == pl.num_programs(1) - 1)
    def _():
        scale = _rms_scale(o_ref, rinv_ref)
        o_ref[...] = x_ref[...] + o_ref[...] * scale * gpost_ref[...]


def _ffn(x2d, gpre, w_up, conv_w, conv_b, w_down, gpost, seq, *, tm=512, tf=512, n_down=2):
    t, d = x2d.shape
    d_ff = w_down.shape[0]
    assert seq % tm == 0 and tm % HALO == 0 and CONV_WIDTH - 1 <= HALO
    assert d_ff % tf == 0 and tf % n_down == 0
    nf = d_ff // tf
    kern = functools.partial(_ffn_kernel, tm=tm, seq=seq, n_down=n_down)
    conv_taps = jnp.concatenate([conv_w, conv_b], axis=0).reshape(CONV_WIDTH + 1, 2 * nf, tf)
    conv_taps = conv_taps.transpose(1, 0, 2)
    return pl.pallas_call(
        kern,
        out_shape=jax.ShapeDtypeStruct((t, d), F32),
        grid=(t // tm, nf),
        in_specs=[
            pl.BlockSpec((tm, d), lambda i, j: (i, 0)),
            pl.BlockSpec((HALO, d), lambda i, j: (jnp.maximum(i * (tm // HALO) - 1, 0), 0)),
            pl.BlockSpec((1, d), lambda i, j: (0, 0)),
            pl.BlockSpec((d, tf), lambda i, j: (0, j)),
            pl.BlockSpec((d, tf), lambda i, j: (0, nf + j)),
            pl.BlockSpec((2 * nf, CONV_WIDTH + 1, tf), lambda i, j: (0, 0, 0)),
            pl.BlockSpec((tf, d), lambda i, j: (j, 0)),
            pl.BlockSpec((1, d), lambda i, j: (0, 0)),
        ],
        out_specs=pl.BlockSpec((tm, d), lambda i, j: (i, 0)),
        scratch_shapes=[pltpu.VMEM((HALO + tm, d), BF16),
                        pltpu.VMEM((2, HALO + tm, tf), F32),
                        pltpu.VMEM((tm, LANES), F32)],
        compiler_params=_params("parallel", "arbitrary"),
        name="ffn",
    )(x2d, x2d, gpre, w_up, w_up, conv_taps, w_down, gpost)


def _ple_kernel(x_ref, p_ref, wple_ref, wgate_ref, gpost_ref, o_ref):
    x = x_ref[...]
    e = jnp.dot(p_ref[...].astype(BF16), wple_ref[...], preferred_element_type=F32)
    gate = jnp.dot(x.astype(BF16), wgate_ref[...], preferred_element_type=F32)
    o_ref[...] = x + _rmsnorm(jax.nn.sigmoid(gate) * e, gpost_ref[...])


def _ple(x2d, p2d, w_ple, w_gate, gpost, *, tm=512):
    t, d = x2d.shape
    pd = p2d.shape[1]
    assert t % tm == 0
    return pl.pallas_call(
        _ple_kernel,
        out_shape=jax.ShapeDtypeStruct((t, d), F32),
        grid=(t // tm,),
        in_specs=[
            pl.BlockSpec((tm, d), lambda i: (i, 0)),
            pl.BlockSpec((tm, pd), lambda i: (i, 0)),
            pl.BlockSpec((pd, d), lambda i: (0, 0)),
            pl.BlockSpec((d, d), lambda i: (0, 0)),
            pl.BlockSpec((1, d), lambda i: (0, 0)),
        ],
        out_specs=pl.BlockSpec((tm, d), lambda i: (i, 0)),
        compiler_params=_params("parallel"),
        name="ple",
    )(x2d, p2d, w_ple, w_gate, gpost)


def kernel(x, p, norm_mix_pre, w_in, w_attn_branch, w_pool_group, pool_scale, w_pool_branch, w_out,
           norm_mix_post, norm_ffn_pre, w_up, conv_w, conv_b, w_down, norm_ffn_post, w_ple, w_ple_gate,
           norm_ple_post):
    batch, seq, d = x.shape
    depth = w_in.shape[0]
    x2d = x.reshape(batch * seq, d)
    row = lambda a: a.reshape(1, -1)
    for l in range(depth):
        proj, (w_up_b,) = _in_proj(x2d, row(norm_mix_pre[l]), w_in[l].astype(BF16), [w_up[l]])
        wpg = w_pool_group[l]
        attn, (wab_b, wpg_b, wpb_b, wout_b, w_down_b) = _attention(
            proj, batch, seq,
            [w_attn_branch[l], wpg.reshape(-1, wpg.shape[-1]), w_pool_branch[l], w_out[l], w_down[l]])
        x2d, (w_gate_b, w_ple_b) = _mix(
            attn, proj, x2d, wab_b, wpg_b.reshape(wpg.shape), row(pool_scale[l]), wpb_b, wout_b,
            row(norm_mix_post[l]), seq, [w_ple_gate[l], w_ple[l]])
        x2d = _ffn(x2d, row(norm_ffn_pre[l]), w_up_b, conv_w[l], row(conv_b[l]), w_down_b,
                   row(norm_ffn_post[l]), seq)
        x2d = _ple(x2d, p[l].reshape(batch * seq, -1), w_ple_b, w_gate_b, row(norm_ple_post[l]))
    return x2d.reshape(batch, seq, d)
```

```python
import functools
import math

import jax
import jax.numpy as jnp
from jax import lax
from jax.experimental import pallas as pl
from jax.experimental.pallas import tpu as pltpu

N_ATTN_HEADS = 8
HEAD_DIM = 128
ATTN_WIDTH = N_ATTN_HEADS * HEAD_DIM
POOL_WINDOWS = (2, 4, 8, 16)
POOL_GROUP_WIDTH = 256
POOL_WIDTH = len(POOL_WINDOWS) * POOL_GROUP_WIDTH
CONV_WIDTH = 3
EPS = 1e-6

LANES = 128
F32_SUBLANES = 8
BF16_SUBLANES = 16

HALO = BF16_SUBLANES

Q_SCALE = HEAD_DIM ** -0.5 * math.log2(math.e)
F32_POW2_UNDERFLOW = 150.0 * (1.0 + 2.0 ** -7)

VMEM_LIMIT_BYTES = 56 * 1024 * 1024

F32 = jnp.float32
BF16 = jnp.bfloat16


def _rmsnorm(x, gain):
    return x * lax.rsqrt(jnp.mean(x * x, axis=-1, keepdims=True) + EPS) * gain


def _rms_scale(src_ref, rinv_ref):
    @pl.when(pl.program_id(0) >= 0)
    def _():
        y = src_ref[...]
        ms = jnp.mean(y * y, axis=-1, keepdims=True)
        rinv_ref[...] = jnp.broadcast_to(lax.rsqrt(ms + EPS), rinv_ref.shape)

    return jnp.concatenate([rinv_ref[...]] * (src_ref.shape[1] // rinv_ref.shape[1]), axis=1)


def _params(*semantics):
    return pltpu.CompilerParams(dimension_semantics=semantics, vmem_limit_bytes=VMEM_LIMIT_BYTES)


def _side_cast_specs(weights, grid, step_of):
    n_steps = math.prod(grid)
    in_specs, out_specs, out_shapes = [], [], []
    for w in weights:
        rows, cols = w.shape
        n_blocks = math.gcd(n_steps, rows // BF16_SUBLANES)
        stride = n_steps // n_blocks
        index = lambda *g, stride=stride: (step_of(*g) // stride, 0)
        in_specs.append(pl.BlockSpec((rows // n_blocks, cols), index))
        out_specs.append(pl.BlockSpec((rows // n_blocks, cols), index))
        out_shapes.append(jax.ShapeDtypeStruct((rows, cols), BF16))
    return in_specs, out_specs, out_shapes


def _cast_sides(src_refs, dst_refs):
    for src, dst in zip(src_refs, dst_refs):
        dst[...] = src[...].astype(BF16)


def _in_proj_kernel(x_ref, g_ref, w_ref, *refs, tn, n_side):
    side_src, (o_ref, *side_dst), h_ref = refs[:n_side], refs[n_side:2 * n_side + 1], refs[-1]
    j = pl.program_id(1)
    _cast_sides(side_src, side_dst)

    @pl.when(j == 0)
    def _():
        h_ref[...] = _rmsnorm(x_ref[...], g_ref[...]).astype(BF16)

    col = j * tn + lax.broadcasted_iota(jnp.int32, (1, tn), 1)
    col_scale = jnp.where(col < ATTN_WIDTH, Q_SCALE, 1.0)
    y = jnp.dot(h_ref[...], w_ref[...], preferred_element_type=F32)
    o_ref[...] = (y * col_scale).astype(o_ref.dtype)


def _in_proj(x2d, gain, w_bf16, side, *, tm=1024, tn=2048):
    t, d = x2d.shape
    n = w_bf16.shape[1]
    assert t % tm == 0 and n % tn == 0
    grid = (t // tm, n // tn)
    side_in, side_out, side_shapes = _side_cast_specs(side, grid, lambda i, j: i * grid[1] + j)
    proj, *side_bf16 = pl.pallas_call(
        functools.partial(_in_proj_kernel, tn=tn, n_side=len(side)),
        out_shape=[jax.ShapeDtypeStruct((t, n), BF16)] + side_shapes,
        grid=grid,
        in_specs=[
            pl.BlockSpec((tm, d), lambda i, j: (i, 0)),
            pl.BlockSpec((1, d), lambda i, j: (0, 0)),
            pl.BlockSpec((d, tn), lambda i, j: (0, j)),
        ] + side_in,
        out_specs=[pl.BlockSpec((tm, tn), lambda i, j: (i, j))] + side_out,
        scratch_shapes=[pltpu.VMEM((tm, d), BF16)],
        compiler_params=_params("arbitrary", "arbitrary"),
        name="in_proj",
    )(x2d, gain, w_bf16, *side)
    return proj, side_bf16


def _attn_kernel(q_ref, k_ref, v_ref, *refs, tq, tk, heads, n_side):
    side_src, (o_ref, *side_dst), (acc_ref, carry_ref) = refs[:n_side], refs[n_side:2 * n_side + 1], refs[-2:]
    qi = pl.program_id(2)
    _cast_sides(side_src, side_dst)

    r = lax.broadcasted_iota(jnp.int32, (tk, 2 * tk), 0)
    c = lax.broadcasted_iota(jnp.int32, (tk, 2 * tk), 1)
    sum_mat = jnp.where(c >= tk, 1.0, jnp.where(r > c, 1.0, 0.0)).astype(BF16)

    def block_sums(nl):
        return jnp.dot(nl.astype(BF16), sum_mat, preferred_element_type=F32)

    def logits(q, k):
        z = lax.dot_general(q, k, (((1,), (1,)), ((), ())), preferred_element_type=F32)
        nl = jnp.maximum(z, 0.0) + jnp.log2(1.0 + jnp.exp2(-jnp.abs(z)))
        return nl, z - nl

    def diagonal(kb):
        ks = pl.multiple_of((kb - 1) * tk, tk)
        row = lax.broadcasted_iota(jnp.int32, (tq, tk), 0)
        col = lax.broadcasted_iota(jnp.int32, (tq, tk), 1)
        causal_lo = col < row
        causal_hi = causal_lo[:tk]
        for h in range(heads):
            dcol = slice(h * HEAD_DIM, (h + 1) * HEAD_DIM)
            ccol = slice(h * tk, (h + 1) * tk)
            nl_lo, w_lo = logits(q_ref[:, dcol], k_ref[pl.ds(ks, tk), dcol])
            nl_hi, w_hi = logits(q_ref[tk:, dcol], k_ref[pl.ds(ks + tk, tk), dcol])
            sums_lo = block_sums(jnp.where(causal_lo, nl_lo, 0.0))
            sums_hi = block_sums(jnp.where(causal_hi, nl_hi, 0.0))
            total_hi = jnp.concatenate([jnp.zeros((tk, tk), F32), sums_hi[:, tk:]], axis=0)
            a_lo = jnp.where(causal_lo, jnp.exp2(w_lo - sums_lo[:, :tk] - total_hi), 0.0)
            a_hi = jnp.where(causal_hi, jnp.exp2(w_hi - sums_hi[:, :tk]), 0.0)
            acc_lo = jnp.dot(a_lo.astype(BF16), v_ref[pl.ds(ks, tk), dcol], preferred_element_type=F32)
            acc_hi = jnp.dot(a_hi.astype(BF16), v_ref[pl.ds(ks + tk, tk), dcol], preferred_element_type=F32)
            acc_ref[:tk, dcol] = acc_lo[:tk]
            acc_ref[tk:, dcol] = acc_lo[tk:] + acc_hi
            carry_ref[:, ccol] = total_hi + sums_lo[:, tk:]

    def pair(kb):
        ks = pl.multiple_of((kb - 1) * tk, tk)
        for h in range(heads):
            dcol = slice(h * HEAD_DIM, (h + 1) * HEAD_DIM)
            ccol = slice(h * tk, (h + 1) * tk)
            nl, w = logits(q_ref[:, dcol], k_ref[pl.ds(ks, 2 * tk), dcol])
            sums_hi = block_sums(nl[:, tk:])
            sums_lo = block_sums(nl[:, :tk])
            carry = carry_ref[:, ccol]
            carry_mid = carry + sums_hi[:, tk:]
            later = jnp.concatenate([sums_lo[:, :tk] + carry_mid, sums_hi[:, :tk] + carry], axis=1)
            a = jnp.exp2(w - later)
            acc_ref[:, dcol] += jnp.dot(a.astype(BF16), v_ref[pl.ds(ks, 2 * tk), dcol],
                                        preferred_element_type=F32)
            carry_ref[:, ccol] = carry_mid + sums_lo[:, tk:]

    assert tq == 2 * tk
    diag_kb = 2 * qi + 1

    @pl.when(qi == 0)
    def _():
        diagonal(diag_kb)

    @pl.when(qi > 0)
    def _():
        diagonal(diag_kb)
        pair(diag_kb - 2)

    def still_live():
        c = carry_ref[...]
        while c.shape[1] > LANES:
            half = c.shape[1] // 2
            c = jnp.minimum(c[:, :half], c[:, half:])
        while c.shape[0] > F32_SUBLANES:
            half = c.shape[0] // 2
            c = jnp.minimum(c[:half], c[half:])
        return (jnp.min(c) <= F32_POW2_UNDERFLOW).astype(jnp.int32)

    def cond(state):
        kb, live = state
        return jnp.logical_and(kb >= 1, live > 0)

    def body(state):
        kb, _ = state
        pair(kb)
        return kb - 2, still_live()

    lax.while_loop(cond, body, (diag_kb - 4, still_live()))
    o_ref[...] = acc_ref[...].astype(o_ref.dtype)


def _attention(proj, batch, seq, side, *, tq=256, tk=128, heads=8):
    t = proj.shape[0]
    assert t == batch * seq and seq % tq == 0 and N_ATTN_HEADS % heads == 0
    nq = seq // tq
    hg = N_ATTN_HEADS // heads
    width = heads * HEAD_DIM
    grid = (batch, hg, nq)
    side_in, side_out, side_shapes = _side_cast_specs(side, grid, lambda b, g, i: (b * hg + g) * nq + i)
    kern = functools.partial(_attn_kernel, tq=tq, tk=tk, heads=heads, n_side=len(side))
    attn, *side_bf16 = pl.pallas_call(
        kern,
        out_shape=[jax.ShapeDtypeStruct((t, ATTN_WIDTH), BF16)] + side_shapes,
        grid=grid,
        in_specs=[
            pl.BlockSpec((tq, width), lambda b, g, i: (b * nq + i, g)),
            pl.BlockSpec((seq, width), lambda b, g, i: (b, hg + g)),
            pl.BlockSpec((seq, width), lambda b, g, i: (b, 2 * hg + g)),
        ] + side_in,
        out_specs=[pl.BlockSpec((tq, width), lambda b, g, i: (b * nq + i, g))] + side_out,
        scratch_shapes=[pltpu.VMEM((tq, width), F32), pltpu.VMEM((tq, heads * tk), F32)],
        compiler_params=_params("arbitrary", "arbitrary", "arbitrary"),
        name="attn",
    )(proj, proj, proj, *side)
    return attn, side_bf16


def _mix_kernel(attn_ref, u_ref, uh_ref, ga_ref, gp_ref, x_ref, wab_ref, wpg_ref, ps_ref, wpb_ref,
                wout_ref, gpost_ref, *refs, tm, seq, n_side):
    side_src, (o_ref, *side_dst) = refs[:n_side], refs[n_side:]
    i = pl.program_id(0)
    _cast_sides(side_src, side_dst)
    seq_pos0 = (i * tm) % seq
    halo = jnp.where(seq_pos0 == 0, 0.0, uh_ref[...].astype(F32))
    ucat = jnp.concatenate([halo, u_ref[...].astype(F32)], axis=0)

    pos = seq_pos0 + lax.broadcasted_iota(jnp.int32, (tm, 1), 0)
    parts = []
    y_attn_parts = []
    n_groups = len(POOL_WINDOWS)
    d = o_ref.shape[1]
    for g, w in enumerate(POOL_WINDOWS):
        acols = slice(g * d // n_groups, (g + 1) * d // n_groups)
        y_attn_parts.append(jnp.dot(attn_ref[...], wab_ref[:, acols], preferred_element_type=F32))
        cols = slice(g * POOL_GROUP_WIDTH, (g + 1) * POOL_GROUP_WIDTH)
        s = ucat[:, cols]
        k = 1
        while k < w:
            s = s + pltpu.roll(s, k, axis=0)
            k *= 2
        count = jnp.minimum(pos + 1, w).astype(F32)
        pooled = s[HALO:] / count - ucat[HALO:, cols]
        pg = jnp.dot(pooled.astype(BF16), wpg_ref[g], preferred_element_type=F32)
        parts.append((pg * ps_ref[:, cols]).astype(BF16))
    y_attn = jnp.concatenate(y_attn_parts, axis=1)
    y_pool = jnp.dot(jnp.concatenate(parts, axis=1), wpb_ref[...], preferred_element_type=F32)
    mixed = (jax.nn.sigmoid(ga_ref[...].astype(F32)) * y_attn
             + jax.nn.sigmoid(gp_ref[...].astype(F32)) * y_pool)
    y = jnp.dot(mixed.astype(BF16), wout_ref[...], preferred_element_type=F32)
    o_ref[...] = x_ref[...] + _rmsnorm(y, gpost_ref[...])


def _mix(attn, proj, x2d, wab, wpg, pool_scale, wpb, wout, gpost, seq, side, *, tm=512):
    t, d = x2d.shape
    assert all(w & (w - 1) == 0 and w <= HALO for w in POOL_WINDOWS)
    assert seq % tm == 0 and tm % HALO == 0
    u_col = 3 * ATTN_WIDTH // POOL_WIDTH
    g_col = (3 * ATTN_WIDTH + POOL_WIDTH) // d
    const = lambda *shape: pl.BlockSpec(shape, lambda i: (0,) * len(shape), pipeline_mode=pl.Buffered(1))
    grid = (t // tm,)
    side_in, side_out, side_shapes = _side_cast_specs(side, grid, lambda i: i)
    kern = functools.partial(_mix_kernel, tm=tm, seq=seq, n_side=len(side))
    x1, *side_bf16 = pl.pallas_call(
        kern,
        out_shape=[jax.ShapeDtypeStruct((t, d), F32)] + side_shapes,
        grid=grid,
        in_specs=[
            pl.BlockSpec((tm, ATTN_WIDTH), lambda i: (i, 0)),
            pl.BlockSpec((tm, POOL_WIDTH), lambda i: (i, u_col)),
            pl.BlockSpec((HALO, POOL_WIDTH), lambda i: (jnp.maximum(i * (tm // HALO) - 1, 0), u_col)),
            pl.BlockSpec((tm, d), lambda i: (i, g_col)),
            pl.BlockSpec((tm, d), lambda i: (i, g_col + 1)),
            pl.BlockSpec((tm, d), lambda i: (i, 0)),
            const(ATTN_WIDTH, d),
            const(len(POOL_WINDOWS), POOL_GROUP_WIDTH, POOL_GROUP_WIDTH),
            const(1, POOL_WIDTH),
            const(POOL_WIDTH, d),
            const(d, d),
            const(1, d),
        ] + side_in,
        out_specs=[pl.BlockSpec((tm, d), lambda i: (i, 0))] + side_out,
        compiler_params=_params("arbitrary"),
        name="mix",
    )(attn, proj, proj, proj, proj, x2d, wab, wpg, pool_scale, wpb, wout, gpost, *side)
    return x1, side_bf16


def _gelu_tanh(x):
    return 0.5 * x * (1.0 + jnp.tanh(0.7978845608028654 * (x + 0.044715 * (x * x * x))))


def _ffn_kernel(x_ref, xh_ref, gpre_ref, wg_ref, wv_ref, conv_ref, wd_ref, gpost_ref, o_ref, h_ref, up_ref,
                rinv_ref, *, tm, seq, n_down):
    i = pl.program_id(0)
    j = pl.program_id(1)
    nf = pl.num_programs(1)

    @pl.when(j == 0)
    def _():
        hh = _rmsnorm(xh_ref[...], gpre_ref[...])
        h_ref[0:HALO, :] = jnp.where((i * tm) % seq == 0, 0.0, hh).astype(BF16)
        h_ref[HALO:, :] = _rmsnorm(x_ref[...], gpre_ref[...]).astype(BF16)
        o_ref[...] = jnp.zeros_like(o_ref)

    h = h_ref[...]

    def conv(slot, w_ref, chunk):
        up_ref[slot] = jnp.dot(h, w_ref[...], preferred_element_type=F32)
        taps = conv_ref[chunk]
        out = taps[CONV_WIDTH:CONV_WIDTH + 1, :]
        for tap in range(CONV_WIDTH):
            off = HALO - (CONV_WIDTH - 1 - tap)
            out = out + up_ref[slot, off:off + tm, :] * taps[tap:tap + 1, :]
        return out

    gate = conv(0, wg_ref, j)
    val = conv(1, wv_ref, nf + j)
    act = (_gelu_tanh(gate) * val).astype(BF16)
    tc = wd_ref.shape[0] // n_down
    for c in range(n_down):
        rows = slice(c * tc, (c + 1) * tc)
        o_ref[...] += jnp.dot(act[:, rows], wd_ref[rows, :], preferred_element_type=F32)

    @pl.when(j == pl.num_programs(1) - 1)
    def _():
        scale = _rms_scale(o_ref, rinv_ref)
        o_ref[...] = x_ref[...] + o_ref[...] * scale * gpost_ref[...]


def _ffn(x2d, gpre, w_up, conv_w, conv_b, w_down, gpost, seq, *, tm=512, tf=512, n_down=2):
    t, d = x2d.shape
    d_ff = w_down.shape[0]
    assert seq % tm == 0 and tm % HALO == 0 and CONV_WIDTH - 1 <= HALO
    assert d_ff % tf == 0 and tf % n_down == 0
    nf = d_ff // tf
    kern = functools.partial(_ffn_kernel, tm=tm, seq=seq, n_down=n_down)
    conv_taps = jnp.concatenate([conv_w, conv_b], axis=0).reshape(CONV_WIDTH + 1, 2 * nf, tf)
    conv_taps = conv_taps.transpose(1, 0, 2)
    return pl.pallas_call(
        kern,
        out_shape=jax.ShapeDtypeStruct((t, d), F32),
        grid=(t // tm, nf),
        in_specs=[
            pl.BlockSpec((tm, d), lambda i, j: (i, 0)),
            pl.BlockSpec((HALO, d), lambda i, j: (jnp.maximum(i * (tm // HALO) - 1, 0), 0)),
            pl.BlockSpec((1, d), lambda i, j: (0, 0)),
            pl.BlockSpec((d, tf), lambda i, j: (0, j)),
            pl.BlockSpec((d, tf), lambda i, j: (0, nf + j)),
            pl.BlockSpec((2 * nf, CONV_WIDTH + 1, tf), lambda i, j: (0, 0, 0)),
            pl.BlockSpec((tf, d), lambda i, j: (j, 0)),
            pl.BlockSpec((1, d), lambda i, j: (0, 0)),
        ],
        out_specs=pl.BlockSpec((tm, d), lambda i, j: (i, 0)),
        scratch_shapes=[pltpu.VMEM((HALO + tm, d), BF16),
                        pltpu.VMEM((2, HALO + tm, tf), F32),
                        pltpu.VMEM((tm, LANES), F32)],
        compiler_params=_params("parallel", "arbitrary"),
        name="ffn",
    )(x2d, x2d, gpre, w_up, w_up, conv_taps, w_down, gpost)


def _ple_kernel(x_ref, p_ref, wple_ref, wgate_ref, gpost_ref, o_ref):
    x = x_ref[...]
    e = jnp.dot(p_ref[...].astype(BF16), wple_ref[...], preferred_element_type=F32)
    gate = jnp.dot(x.astype(BF16), wgate_ref[...], preferred_element_type=F32)
    o_ref[...] = x + _rmsnorm(jax.nn.sigmoid(gate) * e, gpost_ref[...])


def _ple(x2d, p2d, w_ple, w_gate, gpost, *, tm=512):
    t, d = x2d.shape
    pd = p2d.shape[1]
    assert t % tm == 0
    return pl.pallas_call(
        _ple_kernel,
        out_shape=jax.ShapeDtypeStruct((t, d), F32),
        grid=(t // tm,),
        in_specs=[
            pl.BlockSpec((tm, d), lambda i: (i, 0)),
            pl.BlockSpec((tm, pd), lambda i: (i, 0)),
            pl.BlockSpec((pd, d), lambda i: (0, 0)),
            pl.BlockSpec((d, d), lambda i: (0, 0)),
            pl.BlockSpec((1, d), lambda i: (0, 0)),
        ],
        out_specs=pl.BlockSpec((tm, d), lambda i: (i, 0)),
        compiler_params=_params("parallel"),
        name="ple",
    )(x2d, p2d, w_ple, w_gate, gpost)


def kernel(x, p, norm_mix_pre, w_in, w_attn_branch, w_pool_group, pool_scale, w_pool_branch, w_out,
           norm_mix_post, norm_ffn_pre, w_up, conv_w, conv_b, w_down, norm_ffn_post, w_ple, w_ple_gate,
           norm_ple_post):
    batch, seq, d = x.shape
    depth = w_in.shape[0]
    x2d = x.reshape(batch * seq, d)
    row = lambda a: a.reshape(1, -1)
    for l in range(depth):
        proj, (w_up_b,) = _in_proj(x2d, row(norm_mix_pre[l]), w_in[l].astype(BF16), [w_up[l]])
        wpg = w_pool_group[l]
        attn, (wab_b, wpg_b, wpb_b, wout_b, w_down_b) = _attention(
            proj, batch, seq,
            [w_attn_branch[l], wpg.reshape(-1, wpg.shape[-1]), w_pool_branch[l], w_out[l], w_down[l]])
        x2d, (w_gate_b, w_ple_b) = _mix(
            attn, proj, x2d, wab_b, wpg_b.reshape(wpg.shape), row(pool_scale[l]), wpb_b, wout_b,
            row(norm_mix_post[l]), seq, [w_ple_gate[l], w_ple[l]])
        x2d = _ffn(x2d, row(norm_ffn_pre[l]), w_up_b, conv_w[l], row(conv_b[l]), w_down_b,
                   row(norm_ffn_post[l]), seq)
        x2d = _ple(x2d, p[l].reshape(batch * seq, -1), w_ple_b, w_gate_b, row(norm_ple_post[l]))
    return x2d.reshape(batch, seq, d)
```
